```python
import math
import jax, jax.numpy as jnp
from jax import lax
import numpy as np


D_MODEL = 2048
BATCH = 4
SEQ = 8192
DEPTH = 4

N_BRANCH = 4
BRANCH_WIDTH = D_MODEL // 4
LRU_WIDTH = BRANCH_WIDTH
LRU_BLOCKS = 8
LRU_BLOCK = LRU_WIDTH // LRU_BLOCKS
LRU_C = 8.0
CONV_A = 4
FOX_HEADS = 4
FOX_HEAD_DIM = BRANCH_WIDTH // FOX_HEADS
FOX_WIDTH = FOX_HEADS * FOX_HEAD_DIM
Q_BLOCK = 128
SSD_HEADS = 8
SSD_HEAD_DIM = 64
SSD_WIDTH = SSD_HEADS * SSD_HEAD_DIM
SSD_GROUPS = 2
SSD_STATE = 128
SSD_CHUNK = 128
CONV_C = 4
SSD_CONV_DIM = SSD_WIDTH + 2 * SSD_GROUPS * SSD_STATE
RWKV_HEADS = 8
RWKV_HEAD = 64
RWKV_WIDTH = RWKV_HEADS * RWKV_HEAD
RWKV_W_RANK = 64
RWKV_A_RANK = 64
RWKV_G_RANK = 128
RWKV_LN_EPS = 64e-5
RWKV_SPLIT = (RWKV_WIDTH, RWKV_WIDTH, RWKV_WIDTH, RWKV_W_RANK, RWKV_A_RANK, RWKV_G_RANK)
RWKV_PROJ = sum(RWKV_SPLIT)
IN_SPLIT = (N_BRANCH * D_MODEL,
            LRU_WIDTH, LRU_WIDTH,
            FOX_WIDTH, FOX_WIDTH, FOX_WIDTH, FOX_HEADS,
            SSD_WIDTH, SSD_CONV_DIM, SSD_HEADS,
            RWKV_PROJ)
IN_COLS = sum(IN_SPLIT)
D_FF = 5632
CONV_F = 3
PLE_DIM = 256
EPS = 1e-6

kernel_name = 'hybrid_gated_parallel_rglru_fox_ssd_rwkv7'


def _offsets(sizes):
    out, s = [], 0
    for n in sizes[:-1]:
        s += n
        out.append(s)
    return out


def rmsnorm(x, w, eps=EPS):
    xf = x.astype(jnp.float32)
    y = xf * lax.rsqrt(jnp.mean(xf * xf, axis=-1, keepdims=True) + eps)
    return (y * w.astype(jnp.float32)).astype(x.dtype)


def causal_dwconv(x, w, b):
    K, C = w.shape
    y = lax.conv_general_dilated(x, w.astype(x.dtype)[:, None, :], window_strides=(1,),
                                 padding=[(K - 1, 0)], dimension_numbers=('NWC', 'WIO', 'NWC'),
                                 feature_group_count=C)
    return y + b.astype(x.dtype)


def token_shift(u):
    return jnp.pad(u, ((0, 0), (1, 0), (0, 0)))[:, :-1]


def rg_lru_branch(x, gate_x, conv_w, conv_b, w_r, b_r, w_i, b_i, lam):
    Bsz, S, W = x.shape
    xc = causal_dwconv(x, conv_w, conv_b).astype(jnp.float32)
    xb = xc.reshape(Bsz, S, LRU_BLOCKS, LRU_BLOCK)
    r = jax.nn.sigmoid(jnp.einsum('bshi,hij->bshj', xb, w_r).reshape(Bsz, S, W) + b_r)
    i = jax.nn.sigmoid(jnp.einsum('bshi,hij->bshj', xb, w_i).reshape(Bsz, S, W) + b_i)
    log_a = -LRU_C * r * jax.nn.softplus(-lam.astype(jnp.float32))
    a = jnp.exp(log_a)
    u = xc * i * jnp.sqrt(-jnp.expm1(2.0 * log_a))

    def combine(c1, c2):
        a1, b1 = c1
        a2, b2 = c2
        return a1 * a2, a2 * b1 + b2

    _, hseq = lax.associative_scan(combine, (a, u), axis=1)
    return (hseq * jax.nn.gelu(gate_x.astype(jnp.float32))).astype(x.dtype)


def forgetting_attention_branch(q, k, v, f_logit, f_bias, q_gain, k_gain):
    Bsz, S, _ = q.shape
    H, Dh = FOX_HEADS, FOX_HEAD_DIM
    q = rmsnorm(q.reshape(Bsz, S, H, Dh), q_gain)
    k = rmsnorm(k.reshape(Bsz, S, H, Dh), k_gain)
    v = v.reshape(Bsz, S, H, Dh)
    log_f = jax.nn.log_sigmoid(f_logit.astype(jnp.float32) + f_bias.astype(jnp.float32))
    c = jnp.cumsum(log_f, axis=1)
    cT = c.transpose(0, 2, 1)
    nblk = S // Q_BLOCK
    qb = q.reshape(Bsz, nblk, Q_BLOCK, H, Dh).transpose(1, 0, 2, 3, 4)
    cb = c.reshape(Bsz, nblk, Q_BLOCK, H).transpose(1, 0, 3, 2)
    kpos = jnp.arange(S)
    scale = Dh ** -0.5

    def block(args):
        qi, ci, n = args
        s = jnp.einsum('bqhd,bkhd->bhqk', qi, k, preferred_element_type=jnp.float32) * scale
        s = s + (ci[..., :, None] - cT[..., None, :])
        qpos = n * Q_BLOCK + jnp.arange(Q_BLOCK)
        s = jnp.where(kpos[None, :] <= qpos[:, None], s, -jnp.inf)
        pr = jax.nn.softmax(s, axis=-1)
        return jnp.einsum('bhqk,bkhd->bqhd', pr.astype(v.dtype), v)

    o = lax.map(block, (qb, cb, jnp.arange(nblk)))
    return o.transpose(1, 0, 2, 3, 4).reshape(Bsz, S, H * Dh)


def segsum(x):
    T = x.shape[-1]
    cs = jnp.cumsum(x, axis=-1)
    out = cs[..., :, None] - cs[..., None, :]
    return jnp.where(jnp.tril(jnp.ones((T, T), dtype=bool)), out, -jnp.inf)


def ssd_chunked(x, dt, A, bm, cm, d_skip):
    Bsz, S, H, P = x.shape
    L = SSD_CHUNK
    nc = S // L
    rep = H // SSD_GROUPS
    N = bm.shape[-1]
    Bc = jnp.repeat(bm, rep, axis=2).reshape(Bsz, nc, L, H, N)
    Cc = jnp.repeat(cm, rep, axis=2).reshape(Bsz, nc, L, H, N)
    X = (x * dt[..., None]).reshape(Bsz, nc, L, H, P)
    dA = (dt * A).reshape(Bsz, nc, L, H).transpose(0, 3, 1, 2)
    A_cum = jnp.cumsum(dA, axis=-1)
    Lmat = jnp.exp(segsum(dA))
    scores = jnp.einsum('bclhn,bcshn->bhcls', Cc, Bc) * Lmat
    y_diag = jnp.einsum('bhcls,bcshp->bclhp', scores, X)
    decay_states = jnp.exp(A_cum[..., -1:] - A_cum)
    states = jnp.einsum('bcshn,bhcs,bcshp->bchpn', Bc, decay_states, X)
    decay_chunk = jnp.exp(segsum(jnp.pad(A_cum[..., -1], ((0, 0), (0, 0), (1, 0)))))
    states = jnp.pad(states, ((0, 0), (1, 0), (0, 0), (0, 0), (0, 0)))
    states = jnp.einsum('bhzc,bchpn->bzhpn', decay_chunk, states)[:, :-1]
    y_off = jnp.einsum('bclhn,bchpn,bhcl->bclhp', Cc, states, jnp.exp(A_cum))
    return (y_diag + y_off).reshape(Bsz, S, H, P) + x * d_skip.astype(jnp.float32)[:, None]


def mamba2_branch(z, xbc, dt_raw, conv_w, conv_b, dt_bias, a_log, d_skip, norm_w):
    Bsz, S, _ = z.shape
    xbc = jax.nn.silu(causal_dwconv(xbc, conv_w, conv_b)).astype(jnp.float32)
    xs, bm, cm = jnp.split(xbc, [SSD_WIDTH, SSD_WIDTH + SSD_GROUPS * SSD_STATE], axis=-1)
    dt = jax.nn.softplus(dt_raw.astype(jnp.float32) + dt_bias.astype(jnp.float32))
    A = -jnp.exp(a_log.astype(jnp.float32))
    y = ssd_chunked(xs.reshape(Bsz, S, SSD_HEADS, SSD_HEAD_DIM), dt, A,
                    bm.reshape(Bsz, S, SSD_GROUPS, SSD_STATE), cm.reshape(Bsz, S, SSD_GROUPS, SSD_STATE), d_skip)
    y = y.reshape(Bsz, S, SSD_WIDTH) * jax.nn.silu(z.astype(jnp.float32))
    y = rmsnorm(y.reshape(Bsz, S, SSD_GROUPS, -1), norm_w.reshape(SSD_GROUPS, -1)).reshape(Bsz, S, SSD_WIDTH)
    return y.astype(z.dtype)


def rwkv7_branch(u_in, mu, w0, w2, a0, a2, g2, k_k, k_a, r_k, ln_w, ln_b):
    Bsz, S, _ = u_in.shape
    H, N = RWKV_HEADS, RWKV_HEAD
    u = u_in.astype(jnp.float32)
    u = u + (token_shift(u) - u) * mu
    r, k, v, wl, al, gl = jnp.split(u, _offsets(RWKV_SPLIT), axis=-1)
    w = -jax.nn.softplus(-(w0 + jnp.tanh(wl) @ w2)) - 0.5
    decay = jnp.exp(-jnp.exp(w))
    a = jax.nn.sigmoid(a0 + al @ a2)
    g = jax.nn.sigmoid(gl) @ g2

    def hd(t):
        return t.reshape(Bsz, S, H, N)

    kk = hd(k * k_k)
    kk = kk / jnp.maximum(jnp.linalg.norm(kk, axis=-1, keepdims=True), 1e-12)
    k = k * (1.0 + (a - 1.0) * k_a)
    r, decay, k, v, a = hd(r), hd(decay), hd(k), hd(v), hd(a)

    def step(state, inp):
        r_t, w_t, k_t, v_t, kk_t, a_t = inp
        sa = jnp.einsum('bhvk,bhk->bhv', state, -kk_t)
        state = (state * w_t[:, :, None, :] + sa[..., None] * (kk_t * a_t)[:, :, None, :]
                 + v_t[..., None] * k_t[:, :, None, :])
        return state, jnp.einsum('bhvk,bhk->bhv', state, r_t)

    xs = tuple(t.transpose(1, 0, 2, 3) for t in (r, decay, k, v, kk, a))
    state0 = jnp.zeros((Bsz, H, N, N), jnp.float32)
    _, y = lax.scan(step, state0, xs)
    y = y.transpose(1, 0, 2, 3)
    mean = jnp.mean(y, axis=-1, keepdims=True)
    var = jnp.mean(jnp.square(y - mean), axis=-1, keepdims=True)
    y = (y - mean) * lax.rsqrt(var + RWKV_LN_EPS) * ln_w.reshape(H, N) + ln_b.reshape(H, N)
    y = y + jnp.sum(r * k * r_k, axis=-1, keepdims=True) * v
    return (y.reshape(Bsz, S, RWKV_WIDTH) * g).astype(u_in.dtype)


def setup_inputs(seed: int = 0) -> dict:
    key = jax.random.key(seed)
    ks = iter(jax.random.split(key, 64))
    L = DEPTH
    f32 = jnp.float32

    def nrm(shape, scale):
        return scale * jax.random.normal(next(ks), shape, f32)

    def unif(shape, lo, hi):
        return jax.random.uniform(next(ks), shape, f32, lo, hi)

    out = {}
    out['x'] = nrm((BATCH, SEQ, D_MODEL), 1.0)
    out['p'] = nrm((DEPTH, BATCH, SEQ, PLE_DIM), 1.0)
    out['mix_norm'] = 1.0 + nrm((L, D_MODEL), 0.05)
    out['w_in'] = nrm((L, D_MODEL, IN_COLS), D_MODEL ** -0.5)
    out['lru_conv_w'] = nrm((L, CONV_A, LRU_WIDTH), CONV_A ** -0.5)
    out['lru_conv_b'] = nrm((L, LRU_WIDTH), 0.01)
    out['lru_wr'] = nrm((L, LRU_BLOCKS, LRU_BLOCK, LRU_BLOCK), LRU_BLOCK ** -0.5)
    out['lru_br'] = nrm((L, LRU_WIDTH), 0.01)
    out['lru_wi'] = nrm((L, LRU_BLOCKS, LRU_BLOCK, LRU_BLOCK), LRU_BLOCK ** -0.5)
    out['lru_bi'] = nrm((L, LRU_WIDTH), 0.01)
    a_c = unif((L, LRU_WIDTH), 0.9, 0.999) ** (1.0 / LRU_C)
    out['lru_lambda'] = jnp.log(a_c) - jnp.log1p(-a_c)
    out['fox_fbias'] = unif((L, FOX_HEADS), 2.0, 5.0)
    out['fox_q_gain'] = 1.0 + nrm((L, FOX_HEAD_DIM), 0.05)
    out['fox_k_gain'] = 1.0 + nrm((L, FOX_HEAD_DIM), 0.05)
    out['ssd_conv_w'] = nrm((L, CONV_C, SSD_CONV_DIM), CONV_C ** -0.5)
    out['ssd_conv_b'] = nrm((L, SSD_CONV_DIM), 0.01)
    dt0 = jnp.exp(unif((L, SSD_HEADS), math.log(1e-3), math.log(1e-1)))
    out['ssd_dt_bias'] = dt0 + jnp.log(-jnp.expm1(-dt0))
    out['ssd_a_log'] = jnp.log(unif((L, SSD_HEADS), 1.0, 16.0))
    out['ssd_d'] = 1.0 + nrm((L, SSD_HEADS), 0.1)
    out['ssd_norm'] = 1.0 + nrm((L, SSD_WIDTH), 0.05)
    out['rwkv_mu'] = unif((L, RWKV_PROJ), 0.0, 1.0)
    out['rwkv_w0'] = unif((L, RWKV_WIDTH), -6.0, -1.0)
    out['rwkv_w2'] = nrm((L, RWKV_W_RANK, RWKV_WIDTH), 0.5 * RWKV_W_RANK ** -0.5)
    out['rwkv_a0'] = nrm((L, RWKV_WIDTH), 0.1)
    out['rwkv_a2'] = nrm((L, RWKV_A_RANK, RWKV_WIDTH), 0.5 * RWKV_A_RANK ** -0.5)
    out['rwkv_g2'] = nrm((L, RWKV_G_RANK, RWKV_WIDTH), RWKV_G_RANK ** -0.5)
    out['rwkv_kk'] = 0.85 + nrm((L, RWKV_WIDTH), 0.1)
    out['rwkv_ka'] = 1.0 + nrm((L, RWKV_WIDTH), 0.1)
    out['rwkv_rk'] = nrm((L, RWKV_HEADS, RWKV_HEAD), 0.1)
    out['rwkv_ln_w'] = 1.0 + nrm((L, RWKV_WIDTH), 0.05)
    out['rwkv_ln_b'] = nrm((L, RWKV_WIDTH), 0.01)
    out['w_branch'] = nrm((L, N_BRANCH, BRANCH_WIDTH, D_MODEL), BRANCH_WIDTH ** -0.5)
    out['w_out'] = nrm((L, D_MODEL, D_MODEL), D_MODEL ** -0.5)
    out['ffn_norm'] = 1.0 + nrm((L, D_MODEL), 0.05)
    out['w_up'] = nrm((L, D_MODEL, 2 * D_FF), D_MODEL ** -0.5)
    out['ffn_conv_w'] = nrm((L, CONV_F, 2 * D_FF), CONV_F ** -0.5)
    out['ffn_conv_b'] = nrm((L, 2 * D_FF), 0.01)
    out['w_down'] = nrm((L, D_FF, D_MODEL), D_FF ** -0.5)
    out['ple_norm'] = 1.0 + nrm((L, D_MODEL), 0.05)
    out['w_ple_gate'] = nrm((L, D_MODEL, D_MODEL), D_MODEL ** -0.5)
    out['w_ple'] = nrm((L, PLE_DIM, D_MODEL), PLE_DIM ** -0.5)
    return out


def reference(x, p, mix_norm, w_in, lru_conv_w, lru_conv_b, lru_wr, lru_br, lru_wi, lru_bi, lru_lambda,
              fox_fbias, fox_q_gain, fox_k_gain, ssd_conv_w, ssd_conv_b, ssd_dt_bias, ssd_a_log, ssd_d,
              ssd_norm, rwkv_mu, rwkv_w0, rwkv_w2, rwkv_a0, rwkv_a2, rwkv_g2, rwkv_kk, rwkv_ka, rwkv_rk,
              rwkv_ln_w, rwkv_ln_b, w_branch, w_out, ffn_norm, w_up, ffn_conv_w, ffn_conv_b, w_down,
              ple_norm, w_ple_gate, w_ple):
    Bsz, S, D = x.shape
    h = x
    for i in range(DEPTH):
        xn = rmsnorm(h, mix_norm[i])
        proj = jnp.einsum('bsd,dn->bsn', xn, w_in[i])
        (g_raw, a_x, a_g, b_q, b_k, b_v, b_f, c_z, c_xbc, c_dt, d_u) = jnp.split(
            proj, _offsets(IN_SPLIT), axis=-1)
        y_a = rg_lru_branch(a_x, a_g, lru_conv_w[i], lru_conv_b[i], lru_wr[i], lru_br[i],
                            lru_wi[i], lru_bi[i], lru_lambda[i])
        y_b = forgetting_attention_branch(b_q, b_k, b_v, b_f, fox_fbias[i], fox_q_gain[i], fox_k_gain[i])
        y_c = mamba2_branch(c_z, c_xbc, c_dt, ssd_conv_w[i], ssd_conv_b[i], ssd_dt_bias[i],
                            ssd_a_log[i], ssd_d[i], ssd_norm[i])
        y_d = rwkv7_branch(d_u, rwkv_mu[i], rwkv_w0[i], rwkv_w2[i], rwkv_a0[i], rwkv_a2[i], rwkv_g2[i],
                           rwkv_kk[i], rwkv_ka[i], rwkv_rk[i], rwkv_ln_w[i], rwkv_ln_b[i])
        gates = jax.nn.sigmoid(g_raw.astype(jnp.float32)).reshape(Bsz, S, N_BRANCH, D)
        branches = (y_a, y_b, y_c, y_d)
        merged = gates[:, :, 0] * jnp.einsum('bsc,cd->bsd', branches[0], w_branch[i, 0])
        for j in range(1, N_BRANCH):
            merged = merged + gates[:, :, j] * jnp.einsum('bsc,cd->bsd', branches[j], w_branch[i, j])
        h = h + jnp.einsum('bsd,de->bse', merged, w_out[i]).astype(h.dtype)
        hn = rmsnorm(h, ffn_norm[i])
        u = causal_dwconv(jnp.einsum('bsd,df->bsf', hn, w_up[i]), ffn_conv_w[i], ffn_conv_b[i])
        u_gate, u_val = jnp.split(u, 2, axis=-1)
        h = h + jnp.einsum('bsf,fd->bsd', jax.nn.gelu(u_gate) * u_val, w_down[i]).astype(h.dtype)
        pg = jax.nn.sigmoid(jnp.einsum('bsd,de->bse', rmsnorm(h, ple_norm[i]), w_ple_gate[i]))
        h = h + (pg * jnp.einsum('bsq,qd->bsd', p[i], w_ple[i])).astype(h.dtype)
    return h
```

```python
import functools

import jax
import jax.numpy as jnp
from jax import lax
from jax.experimental import pallas as pl
from jax.experimental.pallas import tpu as pltpu

F32 = jnp.float32
BF16 = jnp.bfloat16

D_MODEL = 2048
DEPTH = 4
N_BRANCH = 4
BRANCH_WIDTH = 512
LRU_C = 8.0
FOX_HEADS = 4
FOX_HEAD_DIM = 128
SSD_HEADS = 8
SSD_HEAD_DIM = 64
SSD_GROUPS = 2
SSD_STATE = 128
RWKV_HEADS = 8
RWKV_HEAD = 64
RWKV_LN_EPS = 64e-5
D_FF = 5632
PLE_DIM = 256
EPS = 1e-6

PROJ_COLS = 14336
COL_GATES = 0
COL_AX, COL_AG = 8192, 8704
COL_BQ, COL_BK, COL_BV = 9216, 9728, 10240
COL_CZ, COL_CXBC = 10752, 11264
COL_DR, COL_DK, COL_DV = 12288, 12800, 13312
COL_BF, COL_CDT, COL_DLOW = 13824, 13952, 14080

VMEM_LIMIT = 56 * 1024 * 1024
NEG_BIG = -1e30


def _dot(a, b):
    return jnp.dot(a, b, preferred_element_type=F32)


def _dot_nt(a, b):
    return lax.dot_general(a, b, (((1,), (1,)), ((), ())), preferred_element_type=F32)


def _dot_tn(a, b):
    return lax.dot_general(a, b, (((0,), (0,)), ((), ())), preferred_element_type=F32)


def _bf(x):
    return x.astype(BF16)


def _split3(x):
    x1 = x.astype(BF16)
    r1 = x - x1.astype(F32)
    x2 = r1.astype(BF16)
    x3 = (r1 - x2.astype(F32)).astype(BF16)
    return x1, x2, x3


def _dot_exact_lhs(m, x):
    x1, x2, x3 = _split3(x)
    return _dot(m, x1) + _dot(m, x2) + _dot(m, x3)


def _dot_exact_rhs(x, m):
    x1, x2, x3 = _split3(x)
    return _dot(x1, m) + _dot(x2, m) + _dot(x3, m)


def _softplus(x):
    return jnp.maximum(x, 0.0) + jnp.log1p(jnp.exp(-jnp.abs(x)))


def _sigmoid(x):
    return jax.nn.sigmoid(x)


def _gelu(x):
    return jax.nn.gelu(x)


def _tril_ones(n, dtype=BF16):
    r = lax.broadcasted_iota(jnp.int32, (n, n), 0)
    c = lax.broadcasted_iota(jnp.int32, (n, n), 1)
    return (c <= r).astype(dtype)


def _rms(x, w):
    ms = jnp.mean(x * x, axis=-1, keepdims=True)
    return x * lax.rsqrt(ms + EPS) * w


def _params(sem):
    return pltpu.CompilerParams(dimension_semantics=sem, vmem_limit_bytes=VMEM_LIMIT)


def _norm_matmul_kernel(h_ref, nw_ref, w_ref, o_ref, xn_ref):
    @pl.when(pl.program_id(1) == 0)
    def _():
        xn_ref[...] = _bf(_rms(h_ref[...], nw_ref[...]))

    o_ref[...] = _dot(xn_ref[...], w_ref[...])


def _norm_matmul(h, nw, w, tm, tn):
    T, D = h.shape
    N = w.shape[1]
    return pl.pallas_call(
        _norm_matmul_kernel,
        grid=(T // tm, N // tn),
        in_specs=[
            pl.BlockSpec((tm, D), lambda i, j: (i, 0)),
            pl.BlockSpec((1, D), lambda i, j: (0, 0)),
            pl.BlockSpec((D, tn), lambda i, j: (0, j)),
        ],
        out_specs=pl.BlockSpec((tm, tn), lambda i, j: (i, j)),
        out_shape=jax.ShapeDtypeStruct((T, N), F32),
        scratch_shapes=[pltpu.VMEM((tm, D), BF16)],
        compiler_params=_params(("parallel", "arbitrary")),
        name="in_proj",
    )(h, nw, w)


def _causal_conv(x, halo, cw, cb, width):
    n = x.shape[0]
    xe = jnp.concatenate([halo, x], axis=0)
    y = cb + cw[width - 1:width] * x
    for k in range(width - 1):
        s = width - 1 - k
        y = y + cw[k:k + 1] * xe[8 - s:8 - s + n]
    return y


def _lru_kernel(x_ref, g_ref, cw_ref, cb_ref, wr_ref, br_ref, wi_ref, bi_ref, lam_ref,
                o_ref, halo_ref, st_ref, *, tc):
    @pl.when(pl.program_id(1) == 0)
    def _():
        halo_ref[...] = jnp.zeros_like(halo_ref)
        st_ref[...] = jnp.zeros_like(st_ref)

    x = x_ref[0]
    xc = _causal_conv(x, halo_ref[...], cw_ref[...], cb_ref[...], 4)
    halo_ref[...] = x[tc - 8:]
    xb = _bf(xc)
    r = _sigmoid(_dot(xb, wr_ref[...]) + br_ref[...])
    gi = _sigmoid(_dot(xb, wi_ref[...]) + bi_ref[...])
    log_a = -LRU_C * r * _softplus(-lam_ref[...])
    a = jnp.exp(log_a)
    u = xc * gi * jnp.sqrt(-jnp.tanh(log_a) * (1.0 + a * a))
    row = lax.broadcasted_iota(jnp.int32, (tc, 1), 0)
    u = u + jnp.where(row == 0, a * st_ref[0:1], 0.0)
    d = 1
    while d < tc:
        keep = row >= d
        u_sh = jnp.where(keep, pltpu.roll(u, d, 0), 0.0)
        a_sh = jnp.where(keep, pltpu.roll(a, d, 0), 1.0)
        u = a * u_sh + u
        a = a * a_sh
        d *= 2
    st_ref[0:1] = u[tc - 1:tc]
    o_ref[0] = _bf(u * _gelu(g_ref[0]))


def _lru(proj3, cw, cb, wr, br, wi, bi, lam, tc):
    B, S, _ = proj3.shape
    W = BRANCH_WIDTH
    row = lambda b, c: (0, 0)
    return pl.pallas_call(
        functools.partial(_lru_kernel, tc=tc),
        grid=(B, S // tc),
        in_specs=[
            pl.BlockSpec((1, tc, W), lambda b, c: (b, c, COL_AX // W)),
            pl.BlockSpec((1, tc, W), lambda b, c: (b, c, COL_AG // W)),
            pl.BlockSpec((4, W), row),
            pl.BlockSpec((1, W), row),
            pl.BlockSpec((W, W), row),
            pl.BlockSpec((1, W), row),
            pl.BlockSpec((W, W), row),
            pl.BlockSpec((1, W), row),
            pl.BlockSpec((1, W), row),
        ],
        out_specs=pl.BlockSpec((1, tc, W), lambda b, c: (b, c, 0)),
        out_shape=jax.ShapeDtypeStruct((B, S, W), BF16),
        scratch_shapes=[pltpu.VMEM((8, W), F32), pltpu.VMEM((8, W), F32)],
        compiler_params=_params(("parallel", "arbitrary")),
        name="rg_lru",
    )(proj3, proj3, cw, cb, wr, br, wi, bi, lam)


def _fox_prep_kernel(q_ref, k_ref, v_ref, f_ref, qg_ref, kg_ref, fb_ref,
                     qo_ref, ko_ref, vo_ref, c_ref, ct_ref, off_ref, *, tc):
    @pl.when(pl.program_id(1) == 0)
    def _():
        off_ref[...] = jnp.zeros_like(off_ref)

    scale = FOX_HEAD_DIM ** -0.5
    for h in range(FOX_HEADS):
        sl = slice(h * FOX_HEAD_DIM, (h + 1) * FOX_HEAD_DIM)
        qo_ref[0, :, sl] = _bf(_rms(q_ref[0, :, sl], qg_ref[...]) * scale)
        ko_ref[0, :, sl] = _bf(_rms(k_ref[0, :, sl], kg_ref[...]))
    vo_ref[0] = _bf(v_ref[0])
    log_f = -_softplus(-(f_ref[0] + fb_ref[...]))
    cs = _dot_exact_lhs(_tril_ones(tc), log_f) + off_ref[0:1]
    off_ref[0:1] = cs[tc - 1:tc]
    c_ref[0] = cs
    ct_ref[0] = cs.T[0:8]


def _fox_prep(proj3, qg, kg, fb, tc):
    B, S, _ = proj3.shape
    W = BRANCH_WIDTH
    row = lambda b, c: (0, 0)
    return pl.pallas_call(
        functools.partial(_fox_prep_kernel, tc=tc),
        grid=(B, S // tc),
        in_specs=[
            pl.BlockSpec((1, tc, W), lambda b, c: (b, c, COL_BQ // W)),
            pl.BlockSpec((1, tc, W), lambda b, c: (b, c, COL_BK // W)),
            pl.BlockSpec((1, tc, W), lambda b, c: (b, c, COL_BV // W)),
            pl.BlockSpec((1, tc, 128), lambda b, c: (b, c, COL_BF // 128)),
            pl.BlockSpec((1, FOX_HEAD_DIM), row),
            pl.BlockSpec((1, FOX_HEAD_DIM), row),
            pl.BlockSpec((1, 128), row),
        ],
        out_specs=[
            pl.BlockSpec((1, tc, W), lambda b, c: (b, c, 0)),
            pl.BlockSpec((1, tc, W), lambda b, c: (b, c, 0)),
            pl.BlockSpec((1, tc, W), lambda b, c: (b, c, 0)),
            pl.BlockSpec((1, tc, 128), lambda b, c: (b, c, 0)),
            pl.BlockSpec((1, 8, tc), lambda b, c: (b, 0, c)),
        ],
        out_shape=[
            jax.ShapeDtypeStruct((B, S, W), BF16),
            jax.ShapeDtypeStruct((B, S, W), BF16),
            jax.ShapeDtypeStruct((B, S, W), BF16),
            jax.ShapeDtypeStruct((B, S, 128), F32),
            jax.ShapeDtypeStruct((B, 8, S), F32),
        ],
        scratch_shapes=[pltpu.VMEM((8, 128), F32)],
        compiler_params=_params(("parallel", "arbitrary")),
        name="fox_prep",
    )(proj3, proj3, proj3, proj3, qg, kg, fb)


def _fox_attn_kernel(q_ref, k_ref, v_ref, cq_ref, ck_ref, o_ref, *, tq):
    h = pl.program_id(1)
    qi = pl.program_id(2)
    q = q_ref[0]
    lane = lax.broadcasted_iota(jnp.int32, (tq, 128), 1)
    cq = jnp.sum(jnp.where(lane == h, cq_ref[0], 0.0), axis=1, keepdims=True)

    def scores(ks):
        k = k_ref[0, pl.ds(ks, tq), :]
        ck = ck_ref[0, pl.ds(h, 1), pl.ds(ks, tq)]
        return _dot_nt(q, k) + (cq - ck)

    def update(carry, s, ks):
        m, l, acc = carry
        m_new = jnp.maximum(m, jnp.max(s, axis=1, keepdims=True))
        p = jnp.exp(s - m_new)
        alpha = jnp.exp(m - m_new)
        l = alpha * l + jnp.sum(p, axis=1, keepdims=True)
        acc = alpha * acc + _dot(_bf(p), v_ref[0, pl.ds(ks, tq), :])
        return m_new, l, acc

    def body(ki, carry):
        ks = pl.multiple_of(ki * tq, tq)
        return update(carry, scores(ks), ks)

    init = (jnp.full((tq, 1), NEG_BIG, F32), jnp.zeros((tq, 1), F32),
            jnp.zeros((tq, FOX_HEAD_DIM), F32))
    carry = lax.fori_loop(0, qi, body, init)
    ks = pl.multiple_of(qi * tq, tq)
    r = lax.broadcasted_iota(jnp.int32, (tq, tq), 0)
    c = lax.broadcasted_iota(jnp.int32, (tq, tq), 1)
    s = jnp.where(c <= r, scores(ks), NEG_BIG)
    _, l, acc = update(carry, s, ks)
    o_ref[0] = _bf(acc / l)


def _fox_attn(qn, kn, vb, c, ct, tq):
    B, S, W = qn.shape
    Dh = FOX_HEAD_DIM
    return pl.pallas_call(
        functools.partial(_fox_attn_kernel, tq=tq),
        grid=(B, FOX_HEADS, S // tq),
        in_specs=[
            pl.BlockSpec((1, tq, Dh), lambda b, h, i: (b, i, h)),
            pl.BlockSpec((1, S, Dh), lambda b, h, i: (b, 0, h)),
            pl.BlockSpec((1, S, Dh), lambda b, h, i: (b, 0, h)),
            pl.BlockSpec((1, tq, 128), lambda b, h, i: (b, i, 0)),
            pl.BlockSpec((1, 8, S), lambda b, h, i: (b, 0, 0)),
        ],
        out_specs=pl.BlockSpec((1, tq, Dh), lambda b, h, i: (b, i, h)),
        out_shape=jax.ShapeDtypeStruct((B, S, W), BF16),
        compiler_params=_params(("parallel", "parallel", "arbitrary")),
        name="fox_attn",
    )(qn, kn, vb, c, ct)


def _ssd_kernel(z_ref, xbc_ref, dt_ref, cw_ref, cb_ref, dtb_ref, alog_ref, dsk_ref, nw_ref,
                o_ref, halo_ref, st_ref, *, lc):
    @pl.when(pl.program_id(1) == 0)
    def _():
        halo_ref[...] = jnp.zeros_like(halo_ref)
        st_ref[...] = jnp.zeros_like(st_ref)

    P, N = SSD_HEAD_DIM, SSD_STATE
    W = SSD_HEADS * P
    x = xbc_ref[0]
    xc = _causal_conv(x, halo_ref[...], cw_ref[...], cb_ref[...], 4)
    halo_ref[...] = x[lc - 8:]
    xc = xc * _sigmoid(xc)
    xs = xc[:, :W]
    dt = _softplus(dt_ref[0] + dtb_ref[...])
    a_neg = -jnp.exp(alog_ref[...])
    cum = _dot_exact_lhs(_tril_ones(lc), dt * a_neg)
    cum_t = cum.T
    cum_last = cum[lc - 1:lc]
    r = lax.broadcasted_iota(jnp.int32, (lc, lc), 0)
    c = lax.broadcasted_iota(jnp.int32, (lc, lc), 1)
    tril = c <= r
    rep = SSD_HEADS // SSD_GROUPS
    ys = []
    cb_g = [None] * SSD_GROUPS
    for h in range(SSD_HEADS):
        g = h // rep
        bm = _bf(xc[:, W + g * N:W + (g + 1) * N])
        cm = _bf(xc[:, W + SSD_GROUPS * N + g * N:W + SSD_GROUPS * N + (g + 1) * N])
        if cb_g[g] is None:
            cb_g[g] = _dot_nt(cm, bm)
        cum_col = cum[:, h:h + 1]
        dt_col = dt[:, h:h + 1]
        lmat = jnp.where(tril, jnp.exp(cum_col - cum_t[h:h + 1, :]), 0.0)
        xh = xs[:, h * P:(h + 1) * P]
        xdt = xh * dt_col
        y = _dot(_bf(cb_g[g] * lmat), _bf(xdt))
        st = st_ref[h]
        y = y + _dot(cm, _bf(st)) * jnp.exp(cum_col)
        last = cum_last[:, h:h + 1]
        xdec = xdt * jnp.exp(last - cum_col)
        st_ref[h] = jnp.exp(last) * st + _dot_tn(bm, _bf(xdec))
        ys.append(y + xh * dsk_ref[:, h * P:(h + 1) * P])
    y = jnp.concatenate(ys, axis=1)
    z = z_ref[0]
    y = y * (z * _sigmoid(z))
    gw = W // SSD_GROUPS
    outs = []
    for g in range(SSD_GROUPS):
        outs.append(_rms(y[:, g * gw:(g + 1) * gw], nw_ref[:, g * gw:(g + 1) * gw]))
    o_ref[0] = _bf(jnp.concatenate(outs, axis=1))


def _ssd(proj3, cw, cb, dtb, alog, dsk, nw, lc):
    B, S, _ = proj3.shape
    W = BRANCH_WIDTH
    row = lambda b, c: (0, 0)
    return pl.pallas_call(
        functools.partial(_ssd_kernel, lc=lc),
        grid=(B, S // lc),
        in_specs=[
            pl.BlockSpec((1, lc, W), lambda b, c: (b, c, COL_CZ // W)),
            pl.BlockSpec((1, lc, 1024), lambda b, c: (b, c, COL_CXBC // 1024)),
            pl.BlockSpec((1, lc, 128), lambda b, c: (b, c, COL_CDT // 128)),
            pl.BlockSpec((4, 1024), row),
            pl.BlockSpec((1, 1024), row),
            pl.BlockSpec((1, 128), row),
            pl.BlockSpec((1, 128), row),
            pl.BlockSpec((1, W), row),
            pl.BlockSpec((1, W), row),
        ],
        out_specs=pl.BlockSpec((1, lc, W), lambda b, c: (b, c, 0)),
        out_shape=jax.ShapeDtypeStruct((B, S, W), BF16),
        scratch_shapes=[pltpu.VMEM((8, 1024), F32),
                        pltpu.VMEM((SSD_HEADS, SSD_STATE, SSD_HEAD_DIM), F32)],
        compiler_params=_params(("parallel", "arbitrary")),
        name="ssd",
    )(proj3, proj3, proj3, cw, cb, dtb, alog, dsk, nw)


def _rwkv_kernel(r_ref, k_ref, v_ref, low_ref, mur_ref, muk_ref, muv_ref, mul_ref,
                 w0_ref, w2_ref, a0_ref, a2_ref, g2_ref, kkw_ref, kaw_ref, rk_ref,
                 lnw_ref, lnb_ref, seg_ref, o_ref,
                 pr_ref, pk_ref, pv_ref, pl_ref, st_ref, *, lc):
    @pl.when(pl.program_id(1) == 0)
    def _():
        pr_ref[...] = jnp.zeros_like(pr_ref)
        pk_ref[...] = jnp.zeros_like(pk_ref)
        pv_ref[...] = jnp.zeros_like(pv_ref)
        pl_ref[...] = jnp.zeros_like(pl_ref)
        st_ref[...] = jnp.zeros_like(st_ref)

    L = lc
    N = RWKV_HEAD
    row = lax.broadcasted_iota(jnp.int32, (L, 1), 0)

    def shift_mix(x_ref, prev_ref, mu_ref):
        x = x_ref[0]
        sh = jnp.where(row == 0, prev_ref[0:1], pltpu.roll(x, 1, 0))
        prev_ref[0:1] = x[L - 1:L]
        return x + (sh - x) * mu_ref[...]

    r = shift_mix(r_ref, pr_ref, mur_ref)
    k = shift_mix(k_ref, pk_ref, muk_ref)
    v = shift_mix(v_ref, pv_ref, muv_ref)
    low = shift_mix(low_ref, pl_ref, mul_ref)
    wl, al, gl = low[:, 0:64], low[:, 64:128], low[:, 128:256]

    seg = seg_ref[...]

    def seg_sum(x):
        return _dot_exact_rhs(x, seg)

    w = -_softplus(-(w0_ref[...] + _dot(_bf(jnp.tanh(wl)), w2_ref[...]))) - 0.5
    ld = -jnp.exp(w)
    a = _sigmoid(a0_ref[...] + _dot(_bf(al), a2_ref[...]))
    g = _dot(_bf(_sigmoid(gl)), g2_ref[...])
    kk = k * kkw_ref[...]
    kk = kk / jnp.maximum(jnp.sqrt(seg_sum(kk * kk)), 1e-12)
    kp = k * (1.0 + (a - 1.0) * kaw_ref[...])
    kka = kk * a

    cum = _dot_exact_lhs(_tril_ones(L), ld)
    c_last = cum[L - 1:L]
    e_neg = jnp.exp(-cum)
    e_end = jnp.exp(c_last - cum)
    at = _bf(-kk * jnp.exp(cum - ld))
    rt = r * jnp.exp(cum)
    bt = _bf(kka * e_neg)
    kt = _bf(kp * e_neg)
    bp = _bf(kka * e_end)
    kpe = _bf(kp * e_end)
    w_end = jnp.exp(c_last)
    vb = _bf(v)

    ri = lax.broadcasted_iota(jnp.int32, (L, L), 0)
    ci = lax.broadcasted_iota(jnp.int32, (L, L), 1)
    strict = ci < ri
    incl = ci <= ri
    eye = (ci == ri).astype(F32)

    ys = []
    for h in range(RWKV_HEADS):
        sl = slice(h * N, (h + 1) * N)
        at_h, rt_h, bt_h, kt_h, v_h = at[:, sl], rt[:, sl], bt[:, sl], kt[:, sl], vb[:, sl]
        gm = _dot_nt(jnp.concatenate([at_h, _bf(rt_h)], axis=0),
                     jnp.concatenate([bt_h, kt_h], axis=0))
        a_ab = jnp.where(strict, gm[:L, :L], 0.0)
        a_ak = jnp.where(strict, gm[:L, L:], 0.0)
        a_rb = jnp.where(incl, gm[L:, :L], 0.0)
        a_rk = jnp.where(incl, gm[L:, L:], 0.0)
        pw = _dot(_bf(a_ab), _bf(a_ab))
        tinv = eye + a_ab
        n_done = 2
        while n_done < L:
            pwb = _bf(pw)
            both = _dot(_bf(jnp.concatenate([tinv, pw], axis=0)), pwb)
            tinv = tinv + both[:L]
            pw = both[L:]
            n_done *= 2
        zv = _dot(_bf(a_ak), v_h)
        tz = _dot(_bf(tinv), jnp.concatenate([at_h, _bf(zv)], axis=1))
        at2, uv = tz[:, :N], tz[:, N:]
        mix = _dot(_bf(a_rb), _bf(tz))
        r2 = rt_h + mix[:, :N]
        yv = mix[:, N:] + _dot(_bf(a_rk), v_h)
        m_low = _dot_tn(_bf(at2), bp[:, sl])
        c_add = _dot_tn(jnp.concatenate([_bf(uv), v_h], axis=0),
                        jnp.concatenate([bp[:, sl], kpe[:, sl]], axis=0))
        s0 = st_ref[h]
        s0b = _bf(s0)
        ys.append(_dot_nt(_bf(r2), s0b) + yv)
        st_ref[h] = s0 * w_end[:, sl] + _dot(s0b, _bf(m_low)) + c_add
    y = jnp.concatenate(ys, axis=1)

    inv_n = 1.0 / N
    mean = seg_sum(y) * inv_n
    yc = y - mean
    var = seg_sum(yc * yc) * inv_n
    y = yc * lax.rsqrt(var + RWKV_LN_EPS) * lnw_ref[...] + lnb_ref[...]
    y = y + seg_sum(r * kp * rk_ref[...]) * v
    o_ref[0] = _bf(y * g)


def _rwkv(proj3, mu, w0, w2, a0, a2, g2, kkw, kaw, rk, lnw, lnb, seg, lc):
    B, S, _ = proj3.shape
    W = BRANCH_WIDTH
    row = lambda b, c: (0, 0)
    vec = pl.BlockSpec((1, W), row)
    return pl.pallas_call(
        functools.partial(_rwkv_kernel, lc=lc),
        grid=(B, S // lc),
        in_specs=[
            pl.BlockSpec((1, lc, W), lambda b, c: (b, c, COL_DR // W)),
            pl.BlockSpec((1, lc, W), lambda b, c: (b, c, COL_DK // W)),
            pl.BlockSpec((1, lc, W), lambda b, c: (b, c, COL_DV // W)),
            pl.BlockSpec((1, lc, 256), lambda b, c: (b, c, COL_DLOW // 256)),
            vec, vec, vec, pl.BlockSpec((1, 256), row),
            vec, pl.BlockSpec((64, W), row),
            vec, pl.BlockSpec((64, W), row),
            pl.BlockSpec((128, W), row),
            vec, vec, vec, vec, vec,
            pl.BlockSpec((W, W), row),
        ],
        out_specs=pl.BlockSpec((1, lc, W), lambda b, c: (b, c, 0)),
        out_shape=jax.ShapeDtypeStruct((B, S, W), BF16),
        scratch_shapes=[pltpu.VMEM((8, W), F32), pltpu.VMEM((8, W), F32), pltpu.VMEM((8, W), F32),
                        pltpu.VMEM((8, 256), F32),
                        pltpu.VMEM((RWKV_HEADS, RWKV_HEAD, RWKV_HEAD), F32)],
        compiler_params=_params(("parallel", "arbitrary")),
        name="rwkv7",
    )(proj3, proj3, proj3, proj3, mu[:, 0:512], mu[:, 512:1024], mu[:, 1024:1536], mu[:, 1536:1792],
      w0, w2, a0, a2, g2, kkw, kaw, rk, lnw, lnb, seg)


def _merge_kernel(h_ref, g_ref, ya_ref, yb_ref, yc_ref, yd_ref, wb_ref, wo_ref, o_ref, acc_ref):
    j = pl.program_id(1)
    gate = _sigmoid(g_ref[...])
    for n, y_ref in enumerate((ya_ref, yb_ref, yc_ref, yd_ref)):
        @pl.when(j == n)
        def _(y_ref=y_ref, n=n):
            t = gate * _dot(y_ref[...], wb_ref[0])
            if n == 0:
                acc_ref[...] = t
            else:
                acc_ref[...] += t

    @pl.when(j == N_BRANCH - 1)
    def _():
        o_ref[...] = h_ref[...] + _dot(_bf(acc_ref[...]), wo_ref[...])


def _merge(h, proj, ys, wb, wo, tm):
    T, D = h.shape
    W = BRANCH_WIDTH
    yspec = pl.BlockSpec((tm, W), lambda i, j: (i, 0))
    return pl.pallas_call(
        _merge_kernel,
        grid=(T // tm, N_BRANCH),
        in_specs=[
            pl.BlockSpec((tm, D), lambda i, j: (i, 0)),
            pl.BlockSpec((tm, D), lambda i, j: (i, j)),
            yspec, yspec, yspec, yspec,
            pl.BlockSpec((1, W, D), lambda i, j: (j, 0, 0)),
            pl.BlockSpec((D, D), lambda i, j: (0, 0)),
        ],
        out_specs=pl.BlockSpec((tm, D), lambda i, j: (i, 0)),
        out_shape=jax.ShapeDtypeStruct((T, D), F32),
        scratch_shapes=[pltpu.VMEM((tm, D), F32)],
        compiler_params=_params(("parallel", "arbitrary")),
        name="merge",
    )(h, proj, *ys, wb, wo)


FFN_HALO = 16


def _ffn_kernel(hm_ref, hh_ref, nw_ref, wg_ref, wv_ref, cwg_ref, cbg_ref, cwv_ref, cbv_ref, wd_ref,
                o_ref, hn_ref, acc_ref, *, tm, seq, nf):
    i = pl.program_id(0)
    f = pl.program_id(1)

    @pl.when(f == 0)
    def _():
        hn_ref[FFN_HALO:, :] = _bf(_rms(hm_ref[...], nw_ref[...]))
        keep = ((i * tm) % seq) != 0
        halo = _rms(hh_ref[...], nw_ref[...])
        hn_ref[0:FFN_HALO, :] = _bf(jnp.where(keep, halo, 0.0))

    hn = hn_ref[...]

    def conv(u, cw, cb):
        return (cb + cw[2:3] * u[FFN_HALO:] + cw[1:2] * u[FFN_HALO - 1:FFN_HALO - 1 + tm]
                + cw[0:1] * u[FFN_HALO - 2:FFN_HALO - 2 + tm])

    ug = conv(_dot(hn, wg_ref[...]), cwg_ref[...], cbg_ref[...])
    uv = conv(_dot(hn, wv_ref[...]), cwv_ref[...], cbv_ref[...])
    t = _dot(_bf(_gelu(ug) * uv), wd_ref[...])

    @pl.when(f == 0)
    def _():
        acc_ref[...] = t

    @pl.when(f != 0)
    def _():
        acc_ref[...] += t

    @pl.when(f == nf - 1)
    def _():
        o_ref[...] = hm_ref[...] + acc_ref[...]


def _ffn(h, nw, wup, cw, cb, wd, seq, tm, tf):
    T, D = h.shape
    nf = D_FF // tf
    hb = tm // FFN_HALO
    return pl.pallas_call(
        functools.partial(_ffn_kernel, tm=tm, seq=seq, nf=nf),
        grid=(T // tm, nf),
        in_specs=[
            pl.BlockSpec((tm, D), lambda i, f: (i, 0)),
            pl.BlockSpec((FFN_HALO, D), lambda i, f: (jnp.maximum(i * hb - 1, 0), 0)),
            pl.BlockSpec((1, D), lambda i, f: (0, 0)),
            pl.BlockSpec((D, tf), lambda i, f: (0, f)),
            pl.BlockSpec((D, tf), lambda i, f: (0, f + nf)),
            pl.BlockSpec((3, tf), lambda i, f: (0, f)),
            pl.BlockSpec((1, tf), lambda i, f: (0, f)),
            pl.BlockSpec((3, tf), lambda i, f: (0, f + nf)),
            pl.BlockSpec((1, tf), lambda i, f: (0, f + nf)),
            pl.BlockSpec((tf, D), lambda i, f: (f, 0)),
        ],
        out_specs=pl.BlockSpec((tm, D), lambda i, f: (i, 0)),
        out_shape=jax.ShapeDtypeStruct((T, D), F32),
        scratch_shapes=[pltpu.VMEM((tm + FFN_HALO, D), BF16), pltpu.VMEM((tm, D), F32)],
        compiler_params=_params(("parallel", "arbitrary")),
        name="ffn",
    )(h, h, nw, wup, wup, cw, cb, cw, cb, wd)


def _ple_kernel(h_ref, p_ref, nw_ref, wg_ref, wp_ref, o_ref):
    h = h_ref[...]
    gate = _sigmoid(_dot(_bf(_rms(h, nw_ref[...])), wg_ref[...]))
    o_ref[...] = h + gate * _dot(_bf(p_ref[...]), wp_ref[...])


def _ple(h, p, nw, wg, wp, tm):
    T, D = h.shape
    Q = p.shape[1]
    return pl.pallas_call(
        _ple_kernel,
        grid=(T // tm,),
        in_specs=[
            pl.BlockSpec((tm, D), lambda i: (i, 0)),
            pl.BlockSpec((tm, Q), lambda i: (i, 0)),
            pl.BlockSpec((1, D), lambda i: (0, 0)),
            pl.BlockSpec((D, D), lambda i: (0, 0)),
            pl.BlockSpec((Q, D), lambda i: (0, 0)),
        ],
        out_specs=pl.BlockSpec((tm, D), lambda i: (i, 0)),
        out_shape=jax.ShapeDtypeStruct((T, D), F32),
        compiler_params=_params(("parallel",)),
        name="ple",
    )(h, p, nw, wg, wp)


def _pack_w_in(w):
    D = w.shape[0]
    z = lambda n: jnp.zeros((D, n), w.dtype)
    parts = [w[:, 0:10752], w[:, 10756:11268], w[:, 11268:12292], w[:, 12300:13836],
             w[:, 10752:10756], z(124), w[:, 12292:12300], z(120), w[:, 13836:14092]]
    return _bf(jnp.concatenate(parts, axis=1))


def _block_diag(w):
    nb, n, _ = w.shape
    eye = jnp.eye(nb, dtype=w.dtype)
    return (eye[:, None, :, None] * w[:, :, None, :]).reshape(nb * n, nb * n)


def _pad_row(v, n):
    return jnp.zeros((1, n), F32).at[0, :v.shape[0]].set(v)


def _pick(n, pref):
    return pref if n % pref == 0 else n


def kernel(x, p, mix_norm, w_in, lru_conv_w, lru_conv_b, lru_wr, lru_br, lru_wi, lru_bi, lru_lambda, fox_fbias, fox_q_gain, fox_k_gain, ssd_conv_w, ssd_conv_b, ssd_dt_bias, ssd_a_log, ssd_d, ssd_norm, rwkv_mu, rwkv_w0, rwkv_w2, rwkv_a0, rwkv_a2, rwkv_g2, rwkv_kk, rwkv_ka, rwkv_rk, rwkv_ln_w, rwkv_ln_b, w_branch, w_out, ffn_norm, w_up, ffn_conv_w, ffn_conv_b, w_down, ple_norm, w_ple_gate, w_ple):
    B, S, D = x.shape
    T = B * S
    h = x.reshape(T, D)
    seg = _bf(_block_diag(jnp.ones((RWKV_HEADS, RWKV_HEAD, RWKV_HEAD), F32)))
    r1 = lambda v: v.reshape(1, -1)
    for i in range(DEPTH):
        proj = _norm_matmul(h, r1(mix_norm[i]), _pack_w_in(w_in[i]), _pick(T, 1024), 1024)
        proj3 = proj.reshape(B, S, PROJ_COLS)
        y_a = _lru(proj3, lru_conv_w[i], r1(lru_conv_b[i]), _bf(_block_diag(lru_wr[i])), r1(lru_br[i]),
                   _bf(_block_diag(lru_wi[i])), r1(lru_bi[i]), r1(lru_lambda[i]), _pick(S, 256))
        qn, kn, vb, c, ct = _fox_prep(proj3, r1(fox_q_gain[i]), r1(fox_k_gain[i]),
                                      _pad_row(fox_fbias[i], 128), _pick(S, 512))
        y_b = _fox_attn(qn, kn, vb, c, ct, _pick(S, 512))
        y_c = _ssd(proj3, ssd_conv_w[i], r1(ssd_conv_b[i]), _pad_row(ssd_dt_bias[i], 128),
                   _pad_row(ssd_a_log[i], 128), r1(jnp.repeat(ssd_d[i], SSD_HEAD_DIM)),
                   r1(ssd_norm[i]), _pick(S, 256))
        y_d = _rwkv(proj3, r1(rwkv_mu[i]), r1(rwkv_w0[i]), _bf(rwkv_w2[i]), r1(rwkv_a0[i]),
                    _bf(rwkv_a2[i]), _bf(rwkv_g2[i]), r1(rwkv_kk[i]), r1(rwkv_ka[i]),
                    r1(rwkv_rk[i]), r1(rwkv_ln_w[i]), r1(rwkv_ln_b[i]), seg, 64)
        ys = [y.reshape(T, BRANCH_WIDTH) for y in (y_a, y_b, y_c, y_d)]
        h = _merge(h, proj, ys, _bf(w_branch[i]), _bf(w_out[i]), _pick(T, 256))
        h = _ffn(h, r1(ffn_norm[i]), _bf(w_up[i]), ffn_conv_w[i], r1(ffn_conv_b[i]), _bf(w_down[i]),
                 S, _pick(S, 512), 512)
        h = _ple(h, p[i].reshape(T, PLE_DIM), r1(ple_norm[i]), _bf(w_ple_gate[i]), _bf(w_ple[i]),
                 _pick(T, 512))
    return h.reshape(B, S, D)
```

```python
import functools

import jax
import jax.numpy as jnp
from jax import lax
from jax.experimental import pallas as pl
from jax.experimental.pallas import tpu as pltpu

F32 = jnp.float32
BF16 = jnp.bfloat16

D_MODEL = 2048
DEPTH = 4
N_BRANCH = 4
BRANCH_WIDTH = 512
LRU_C = 8.0
FOX_HEADS = 4
FOX_HEAD_DIM = 128
SSD_HEADS = 8
SSD_HEAD_DIM = 64
SSD_GROUPS = 2
SSD_STATE = 128
RWKV_HEADS = 8
RWKV_HEAD = 64
RWKV_LN_EPS = 64e-5
D_FF = 5632
PLE_DIM = 256
EPS = 1e-6

IN_COLS = 14092
PROJ_COLS = 14336
COL_GATES = 0
COL_AX, COL_AG = 8192, 8704
COL_BQ, COL_BK, COL_BV = 9216, 9728, 10240
COL_BF = 10752
COL_C = 10752
COL_D = 12288

VMEM_LIMIT = 56 * 1024 * 1024
NEG_BIG = -1e30


def _dot(a, b):
    return jnp.dot(a, b, preferred_element_type=F32)


def _dot_nt(a, b):
    return lax.dot_general(a, b, (((1,), (1,)), ((), ())), preferred_element_type=F32)


def _dot_tn(a, b):
    return lax.dot_general(a, b, (((0,), (0,)), ((), ())), preferred_element_type=F32)


def _bf(x):
    return x.astype(BF16)


def _split3(x):
    x1 = x.astype(BF16)
    r1 = x - x1.astype(F32)
    x2 = r1.astype(BF16)
    x3 = (r1 - x2.astype(F32)).astype(BF16)
    return x1, x2, x3


def _dot_exact_lhs(m, x):
    x1, x2, x3 = _split3(x)
    return _dot(m, x1) + _dot(m, x2) + _dot(m, x3)


def _dot_exact_rhs(x, m):
    x1, x2, x3 = _split3(x)
    return _dot(x1, m) + _dot(x2, m) + _dot(x3, m)


def _softplus(x):
    return jnp.maximum(x, 0.0) + jnp.log1p(jnp.exp(-jnp.abs(x)))


def _sigmoid(x):
    return jax.nn.sigmoid(x)


def _gelu(x):
    return jax.nn.gelu(x)


def _tril_ones(n, dtype=BF16):
    r = lax.broadcasted_iota(jnp.int32, (n, n), 0)
    c = lax.broadcasted_iota(jnp.int32, (n, n), 1)
    return (c <= r).astype(dtype)


def _rms(x, w):
    ms = jnp.mean(x * x, axis=-1, keepdims=True)
    return x * lax.rsqrt(ms + EPS) * w


def _params(sem):
    return pltpu.CompilerParams(dimension_semantics=sem, vmem_limit_bytes=VMEM_LIMIT)


def _norm_matmul_kernel(h_ref, nw_ref, w_ref, o_ref, xn_ref):
    @pl.when(pl.program_id(1) == 0)
    def _():
        xn_ref[...] = _bf(_rms(h_ref[...], nw_ref[...]))

    o_ref[...] = _dot(xn_ref[...], w_ref[...])


def _norm_matmul(h, nw, w, tm, tn):
    T, D = h.shape
    N = w.shape[1]
    return pl.pallas_call(
        _norm_matmul_kernel,
        grid=(T // tm, N // tn),
        in_specs=[
            pl.BlockSpec((tm, D), lambda i, j: (i, 0)),
            pl.BlockSpec((1, D), lambda i, j: (0, 0)),
            pl.BlockSpec((D, tn), lambda i, j: (0, j)),
        ],
        out_specs=pl.BlockSpec((tm, tn), lambda i, j: (i, j)),
        out_shape=jax.ShapeDtypeStruct((T, N), F32),
        scratch_shapes=[pltpu.VMEM((tm, D), BF16)],
        compiler_params=_params(("parallel", "arbitrary")),
        name="in_proj",
    )(h, nw, w)


def _causal_conv(x, halo, cw, cb, width):
    n = x.shape[0]
    xe = jnp.concatenate([halo, x], axis=0)
    y = cb + cw[width - 1:width] * x
    for k in range(width - 1):
        s = width - 1 - k
        y = y + cw[k:k + 1] * xe[8 - s:8 - s + n]
    return y


def _lru_kernel(x_ref, g_ref, cw_ref, cb_ref, wr_ref, br_ref, wi_ref, bi_ref, lam_ref,
                o_ref, halo_ref, st_ref, *, tc):
    @pl.when(pl.program_id(1) == 0)
    def _():
        halo_ref[...] = jnp.zeros_like(halo_ref)
        st_ref[...] = jnp.zeros_like(st_ref)

    x = x_ref[0]
    xc = _causal_conv(x, halo_ref[...], cw_ref[...], cb_ref[...], 4)
    halo_ref[...] = x[tc - 8:]
    xb = _bf(xc)
    r = _sigmoid(_dot(xb, wr_ref[...]) + br_ref[...])
    gi = _sigmoid(_dot(xb, wi_ref[...]) + bi_ref[...])
    log_a = -LRU_C * r * _softplus(-lam_ref[...])
    a = jnp.exp(log_a)
    u = xc * gi * jnp.sqrt(-jnp.tanh(log_a) * (1.0 + a * a))
    row = lax.broadcasted_iota(jnp.int32, (tc, 1), 0)
    u = u + jnp.where(row == 0, a * st_ref[0:1], 0.0)
    d = 1
    while d < tc:
        keep = row >= d
        u_sh = jnp.where(keep, pltpu.roll(u, d, 0), 0.0)
        a_sh = jnp.where(keep, pltpu.roll(a, d, 0), 1.0)
        u = a * u_sh + u
        a = a * a_sh
        d *= 2
    st_ref[0:1] = u[tc - 1:tc]
    o_ref[0] = _bf(u * _gelu(g_ref[0]))


def _lru(proj3, cw, cb, wr, br, wi, bi, lam, tc):
    B, S, _ = proj3.shape
    W = BRANCH_WIDTH
    row = lambda b, c: (0, 0)
    return pl.pallas_call(
        functools.partial(_lru_kernel, tc=tc),
        grid=(B, S // tc),
        in_specs=[
            pl.BlockSpec((1, tc, W), lambda b, c: (b, c, COL_AX // W)),
            pl.BlockSpec((1, tc, W), lambda b, c: (b, c, COL_AG // W)),
            pl.BlockSpec((4, W), row),
            pl.BlockSpec((1, W), row),
            pl.BlockSpec((W, W), row),
            pl.BlockSpec((1, W), row),
            pl.BlockSpec((W, W), row),
            pl.BlockSpec((1, W), row),
            pl.BlockSpec((1, W), row),
        ],
        out_specs=pl.BlockSpec((1, tc, W), lambda b, c: (b, c, 0)),
        out_shape=jax.ShapeDtypeStruct((B, S, W), BF16),
        scratch_shapes=[pltpu.VMEM((8, W), F32), pltpu.VMEM((8, W), F32)],
        compiler_params=_params(("parallel", "arbitrary")),
        name="rg_lru",
    )(proj3, proj3, cw, cb, wr, br, wi, bi, lam)


def _fox_prep_kernel(q_ref, k_ref, v_ref, f_ref, qg_ref, kg_ref, fb_ref,
                     qo_ref, ko_ref, vo_ref, ct_ref, off_ref, *, tc):
    @pl.when(pl.program_id(1) == 0)
    def _():
        off_ref[...] = jnp.zeros_like(off_ref)

    scale = FOX_HEAD_DIM ** -0.5
    for h in range(FOX_HEADS):
        sl = slice(h * FOX_HEAD_DIM, (h + 1) * FOX_HEAD_DIM)
        qo_ref[0, :, sl] = _bf(_rms(q_ref[0, :, sl], qg_ref[...]) * scale)
        ko_ref[0, :, sl] = _bf(_rms(k_ref[0, :, sl], kg_ref[...]))
    vo_ref[0] = _bf(v_ref[0])
    log_f = -_softplus(-(f_ref[0] + fb_ref[...]))
    cs = _dot_exact_lhs(_tril_ones(tc), log_f) + off_ref[0:1]
    off_ref[0:1] = cs[tc - 1:tc]
    ct_ref[0] = cs.T[0:8]


def _fox_prep(proj3, qg, kg, fb, tc):
    B, S, _ = proj3.shape
    W = BRANCH_WIDTH
    row = lambda b, c: (0, 0)
    return pl.pallas_call(
        functools.partial(_fox_prep_kernel, tc=tc),
        grid=(B, S // tc),
        in_specs=[
            pl.BlockSpec((1, tc, W), lambda b, c: (b, c, COL_BQ // W)),
            pl.BlockSpec((1, tc, W), lambda b, c: (b, c, COL_BK // W)),
            pl.BlockSpec((1, tc, W), lambda b, c: (b, c, COL_BV // W)),
            pl.BlockSpec((1, tc, 128), lambda b, c: (b, c, COL_BF // 128)),
            pl.BlockSpec((1, FOX_HEAD_DIM), row),
            pl.BlockSpec((1, FOX_HEAD_DIM), row),
            pl.BlockSpec((1, 128), row),
        ],
        out_specs=[
            pl.BlockSpec((1, tc, W), lambda b, c: (b, c, 0)),
            pl.BlockSpec((1, tc, W), lambda b, c: (b, c, 0)),
            pl.BlockSpec((1, tc, W), lambda b, c: (b, c, 0)),
            pl.BlockSpec((1, 8, tc), lambda b, c: (b, 0, c)),
        ],
        out_shape=[
            jax.ShapeDtypeStruct((B, S, W), BF16),
            jax.ShapeDtypeStruct((B, S, W), BF16),
            jax.ShapeDtypeStruct((B, S, W), BF16),
            jax.ShapeDtypeStruct((B, 8, S), F32),
        ],
        scratch_shapes=[pltpu.VMEM((8, 128), F32)],
        compiler_params=_params(("parallel", "arbitrary")),
        name="fox_prep",
    )(proj3, proj3, proj3, proj3, qg, kg, fb)


def _fox_attn_kernel(q_ref, k_ref, v_ref, ck_ref, o_ref, *, tq):
    h = pl.program_id(1)
    qi = pl.program_id(2)
    q = q_ref[0]

    def scores(ks):
        k = k_ref[0, pl.ds(ks, tq), :]
        ck = ck_ref[0, pl.ds(h, 1), pl.ds(ks, tq)]
        return _dot_nt(q, k) - ck

    def update(carry, s, ks):
        m, l, acc = carry
        m_new = jnp.maximum(m, jnp.max(s, axis=1, keepdims=True))
        p = jnp.exp(s - m_new)
        alpha = jnp.exp(m - m_new)
        l = alpha * l + jnp.sum(p, axis=1, keepdims=True)
        acc = alpha * acc + _dot(_bf(p), v_ref[0, pl.ds(ks, tq), :])
        return m_new, l, acc

    def body(ki, carry):
        ks = pl.multiple_of(ki * tq, tq)
        return update(carry, scores(ks), ks)

    init = (jnp.full((tq, 1), NEG_BIG, F32), jnp.zeros((tq, 1), F32),
            jnp.zeros((tq, FOX_HEAD_DIM), F32))
    carry = lax.fori_loop(0, qi, body, init)
    ks = pl.multiple_of(qi * tq, tq)
    r = lax.broadcasted_iota(jnp.int32, (tq, tq), 0)
    c = lax.broadcasted_iota(jnp.int32, (tq, tq), 1)
    s = jnp.where(c <= r, scores(ks), NEG_BIG)
    _, l, acc = update(carry, s, ks)
    o_ref[0] = _bf(acc / l)


def _fox_attn(qn, kn, vb, ct, tq):
    B, S, W = qn.shape
    Dh = FOX_HEAD_DIM
    return pl.pallas_call(
        functools.partial(_fox_attn_kernel, tq=tq),
        grid=(B, FOX_HEADS, S // tq),
        in_specs=[
            pl.BlockSpec((1, tq, Dh), lambda b, h, i: (b, i, h)),
            pl.BlockSpec((1, S, Dh), lambda b, h, i: (b, 0, h)),
            pl.BlockSpec((1, S, Dh), lambda b, h, i: (b, 0, h)),
            pl.BlockSpec((1, 8, S), lambda b, h, i: (b, 0, 0)),
        ],
        out_specs=pl.BlockSpec((1, tq, Dh), lambda b, h, i: (b, i, h)),
        out_shape=jax.ShapeDtypeStruct((B, S, W), BF16),
        compiler_params=_params(("parallel", "parallel", "arbitrary")),
        name="fox_attn",
    )(qn, kn, vb, ct)


SSD_BLOCK = 1792
SSD_SHIFT = 4


def _ssd_kernel(u_ref, cw_ref, cb_ref, dtb_ref, alog_ref, dsk_ref, nw_ref,
                o_ref, halo_ref, st_ref, *, lc):
    @pl.when(pl.program_id(1) == 0)
    def _():
        halo_ref[...] = jnp.zeros_like(halo_ref)
        st_ref[...] = jnp.zeros_like(st_ref)

    P, N = SSD_HEAD_DIM, SSD_STATE
    W = SSD_HEADS * P
    u = pltpu.roll(u_ref[0], SSD_BLOCK - SSD_SHIFT, 1)
    z = u[:, 0:W]
    x = u[:, W:W + 1024]
    dt_raw = u[:, W + 1024:W + 1152]
    xc = _causal_conv(x, halo_ref[...], cw_ref[...], cb_ref[...], 4)
    halo_ref[...] = x[lc - 8:]
    xc = xc * _sigmoid(xc)
    xs = xc[:, :W]
    dt = _softplus(dt_raw + dtb_ref[...])
    a_neg = -jnp.exp(alog_ref[...])
    cum = _dot_exact_lhs(_tril_ones(lc), dt * a_neg)
    cum_t = cum.T
    cum_last = cum[lc - 1:lc]
    r = lax.broadcasted_iota(jnp.int32, (lc, lc), 0)
    c = lax.broadcasted_iota(jnp.int32, (lc, lc), 1)
    tril = c <= r
    rep = SSD_HEADS // SSD_GROUPS
    ys = []
    cb_g = [None] * SSD_GROUPS
    for h in range(SSD_HEADS):
        g = h // rep
        bm = _bf(xc[:, W + g * N:W + (g + 1) * N])
        cm = _bf(xc[:, W + SSD_GROUPS * N + g * N:W + SSD_GROUPS * N + (g + 1) * N])
        if cb_g[g] is None:
            cb_g[g] = _dot_nt(cm, bm)
        cum_col = cum[:, h:h + 1]
        dt_col = dt[:, h:h + 1]
        lmat = jnp.where(tril, jnp.exp(cum_col - cum_t[h:h + 1, :]), 0.0)
        xh = xs[:, h * P:(h + 1) * P]
        xdt = xh * dt_col
        y = _dot(_bf(cb_g[g] * lmat), _bf(xdt))
        st = st_ref[h]
        y = y + _dot(cm, _bf(st)) * jnp.exp(cum_col)
        last = cum_last[:, h:h + 1]
        xdec = xdt * jnp.exp(last - cum_col)
        st_ref[h] = jnp.exp(last) * st + _dot_tn(bm, _bf(xdec))
        ys.append(y + xh * dsk_ref[:, h * P:(h + 1) * P])
    y = jnp.concatenate(ys, axis=1)
    y = y * (z * _sigmoid(z))
    gw = W // SSD_GROUPS
    outs = []
    for g in range(SSD_GROUPS):
        outs.append(_rms(y[:, g * gw:(g + 1) * gw], nw_ref[:, g * gw:(g + 1) * gw]))
    o_ref[0] = _bf(jnp.concatenate(outs, axis=1))


def _ssd(proj3, cw, cb, dtb, alog, dsk, nw, lc):
    B, S, _ = proj3.shape
    W = BRANCH_WIDTH
    row = lambda b, c: (0, 0)
    return pl.pallas_call(
        functools.partial(_ssd_kernel, lc=lc),
        grid=(B, S // lc),
        in_specs=[
            pl.BlockSpec((1, lc, SSD_BLOCK), lambda b, c: (b, c, COL_C // SSD_BLOCK)),
            pl.BlockSpec((4, 1024), row),
            pl.BlockSpec((1, 1024), row),
            pl.BlockSpec((1, 128), row),
            pl.BlockSpec((1, 128), row),
            pl.BlockSpec((1, W), row),
            pl.BlockSpec((1, W), row),
        ],
        out_specs=pl.BlockSpec((1, lc, W), lambda b, c: (b, c, 0)),
        out_shape=jax.ShapeDtypeStruct((B, S, W), BF16),
        scratch_shapes=[pltpu.VMEM((8, 1024), F32),
                        pltpu.VMEM((SSD_HEADS, SSD_STATE, SSD_HEAD_DIM), F32)],
        compiler_params=_params(("parallel", "arbitrary")),
        name="ssd",
    )(proj3, cw, cb, dtb, alog, dsk, nw)


RWKV_BLOCK = 2048
RWKV_SHIFT = 12


def _rwkv_kernel(u_ref, mu_ref, w0_ref, w2_ref, a0_ref, a2_ref, g2_ref, kkw_ref, kaw_ref, rk_ref,
                 lnw_ref, lnb_ref, seg_ref, o_ref, prev_ref, st_ref, *, lc):
    @pl.when(pl.program_id(1) == 0)
    def _():
        prev_ref[...] = jnp.zeros_like(prev_ref)
        st_ref[...] = jnp.zeros_like(st_ref)

    L = lc
    N = RWKV_HEAD
    HEADS = range(RWKV_HEADS)
    W = RWKV_HEADS * N
    row = lax.broadcasted_iota(jnp.int32, (L, 1), 0)

    x = pltpu.roll(u_ref[0], RWKV_BLOCK - RWKV_SHIFT, 1)[:, :3 * W + 256]
    sh = jnp.where(row == 0, prev_ref[0:1], pltpu.roll(x, 1, 0))
    prev_ref[0:1] = x[L - 1:L]
    u = x + (sh - x) * mu_ref[...]
    r, k, v = u[:, 0:W], u[:, W:2 * W], u[:, 2 * W:3 * W]
    wl, al, gl = u[:, 3 * W:3 * W + 64], u[:, 3 * W + 64:3 * W + 128], u[:, 3 * W + 128:3 * W + 256]

    seg = seg_ref[...]

    def seg_sum(x):
        return _dot_exact_rhs(x, seg)

    w = -_softplus(-(w0_ref[...] + _dot(_bf(jnp.tanh(wl)), w2_ref[...]))) - 0.5
    ld = -jnp.exp(w)
    a = _sigmoid(a0_ref[...] + _dot(_bf(al), a2_ref[...]))
    g = _dot(_bf(_sigmoid(gl)), g2_ref[...])
    kk = k * kkw_ref[...]
    kk = kk / jnp.maximum(jnp.sqrt(seg_sum(kk * kk)), 1e-12)
    kp = k * (1.0 + (a - 1.0) * kaw_ref[...])
    kka = kk * a

    cum = _dot_exact_lhs(_tril_ones(L), ld)
    c_last = cum[L - 1:L]
    e_neg = jnp.exp(-cum)
    e_end = jnp.exp(c_last - cum)
    at = _bf(-kk * jnp.exp(cum - ld))
    rt = r * jnp.exp(cum)
    rtb = _bf(rt)
    bt = _bf(kka * e_neg)
    kt = _bf(kp * e_neg)
    bp = _bf(kka * e_end)
    kpe = _bf(kp * e_end)
    w_end = jnp.exp(c_last)
    vb = _bf(v)

    ri = lax.broadcasted_iota(jnp.int32, (L, L), 0)
    ci = lax.broadcasted_iota(jnp.int32, (L, L), 1)
    strict = ci < ri
    incl = ci <= ri
    eye = (ci == ri).astype(F32)
    sl = [slice(h * N, (h + 1) * N) for h in HEADS]

    gm = [_dot_nt(jnp.concatenate([at[:, sl[h]], rtb[:, sl[h]]], axis=0),
                  jnp.concatenate([bt[:, sl[h]], kt[:, sl[h]]], axis=0)) for h in HEADS]
    a_ab = [_bf(jnp.where(strict, gm[h][:L, :L], 0.0)) for h in HEADS]
    a_ak = [_bf(jnp.where(strict, gm[h][:L, L:], 0.0)) for h in HEADS]
    a_rb = [_bf(jnp.where(incl, gm[h][L:, :L], 0.0)) for h in HEADS]
    a_rk = [_bf(jnp.where(incl, gm[h][L:, L:], 0.0)) for h in HEADS]
    pw = [_dot(a_ab[h], a_ab[h]) for h in HEADS]
    tinv = [eye + a_ab[h].astype(F32) for h in HEADS]
    zv = [_dot(a_ak[h], vb[:, sl[h]]) for h in HEADS]
    n_done = 2
    while 2 * n_done < L:
        both = [_dot(_bf(jnp.concatenate([tinv[h], pw[h]], axis=0)), _bf(pw[h])) for h in HEADS]
        tinv = [tinv[h] + both[h][:L] for h in HEADS]
        pw = [both[h][L:] for h in HEADS]
        n_done *= 2
    tinv = [tinv[h] + _dot(_bf(tinv[h]), _bf(pw[h])) for h in HEADS]
    tz = [_dot(_bf(tinv[h]), jnp.concatenate([at[:, sl[h]], _bf(zv[h])], axis=1)) for h in HEADS]
    tzb = [_bf(tz[h]) for h in HEADS]
    mix = [_dot(a_rb[h], tzb[h]) for h in HEADS]
    yv = [mix[h][:, N:] + _dot(a_rk[h], vb[:, sl[h]]) for h in HEADS]
    r2 = [_bf(rt[:, sl[h]] + mix[h][:, :N]) for h in HEADS]
    m_low = [_bf(_dot_tn(tzb[h][:, :N], bp[:, sl[h]])) for h in HEADS]
    c_add = [_dot_tn(jnp.concatenate([tzb[h][:, N:], vb[:, sl[h]]], axis=0),
                     jnp.concatenate([bp[:, sl[h]], kpe[:, sl[h]]], axis=0)) for h in HEADS]
    s0 = [st_ref[h] for h in HEADS]
    s0b = [_bf(s0[h]) for h in HEADS]
    ys = [_dot_nt(r2[h], s0b[h]) + yv[h] for h in HEADS]
    for h in HEADS:
        st_ref[h] = s0[h] * w_end[:, sl[h]] + _dot(s0b[h], m_low[h]) + c_add[h]
    y = jnp.concatenate(ys, axis=1)

    inv_n = 1.0 / N
    mean = seg_sum(y) * inv_n
    yc = y - mean
    var = seg_sum(yc * yc) * inv_n
    y = yc * lax.rsqrt(var + RWKV_LN_EPS) * lnw_ref[...] + lnb_ref[...]
    y = y + seg_sum(r * kp * rk_ref[...]) * v
    o_ref[0] = _bf(y * g)


def _rwkv(proj3, mu, w0, w2, a0, a2, g2, kkw, kaw, rk, lnw, lnb, seg, lc):
    B, S, _ = proj3.shape
    W = BRANCH_WIDTH
    row = lambda b, c: (0, 0)
    vec = pl.BlockSpec((1, W), row)
    return pl.pallas_call(
        functools.partial(_rwkv_kernel, lc=lc),
        grid=(B, S // lc),
        in_specs=[
            pl.BlockSpec((1, lc, RWKV_BLOCK), lambda b, c: (b, c, COL_D // RWKV_BLOCK)),
            pl.BlockSpec((1, 3 * W + 256), row),
            vec, pl.BlockSpec((64, W), row),
            vec, pl.BlockSpec((64, W), row),
            pl.BlockSpec((128, W), row),
            vec, vec, vec, vec, vec,
            pl.BlockSpec((W, W), row),
        ],
        out_specs=pl.BlockSpec((1, lc, W), lambda b, c: (b, c, 0)),
        out_shape=jax.ShapeDtypeStruct((B, S, W), BF16),
        scratch_shapes=[pltpu.VMEM((8, 3 * W + 256), F32),
                        pltpu.VMEM((RWKV_HEADS, RWKV_HEAD, RWKV_HEAD), F32)],
        compiler_params=_params(("parallel", "arbitrary")),
        name="rwkv7",
    )(proj3, mu, w0, w2, a0, a2, g2, kkw, kaw, rk, lnw, lnb, seg)


def _merge_kernel(h_ref, g_ref, ya_ref, yb_ref, yc_ref, yd_ref, wb_ref, wo_ref, o_ref):
    D = D_MODEL
    acc = None
    for n, y_ref in enumerate((ya_ref, yb_ref, yc_ref, yd_ref)):
        t = _sigmoid(g_ref[:, n * D:(n + 1) * D]) * _dot(y_ref[...], wb_ref[n])
        acc = t if acc is None else acc + t
    o_ref[...] = h_ref[...] + _dot(_bf(acc), wo_ref[...])


def _merge(h, proj, ys, wb, wo, tm):
    T, D = h.shape
    W = BRANCH_WIDTH
    yspec = pl.BlockSpec((tm, W), lambda i: (i, 0))
    once = pl.Buffered(1)
    return pl.pallas_call(
        _merge_kernel,
        grid=(T // tm,),
        in_specs=[
            pl.BlockSpec((tm, D), lambda i: (i, 0)),
            pl.BlockSpec((tm, N_BRANCH * D), lambda i: (i, 0)),
            yspec, yspec, yspec, yspec,
            pl.BlockSpec((N_BRANCH, W, D), lambda i: (0, 0, 0), pipeline_mode=once),
            pl.BlockSpec((D, D), lambda i: (0, 0), pipeline_mode=once),
        ],
        out_specs=pl.BlockSpec((tm, D), lambda i: (i, 0)),
        out_shape=jax.ShapeDtypeStruct((T, D), F32),
        compiler_params=_params(("parallel",)),
        name="merge",
    )(h, proj, *ys, wb, wo)


FFN_HALO = 16


def _ffn_kernel(hm_ref, hh_ref, nw_ref, wg_ref, wv_ref, cwg_ref, cbg_ref, cwv_ref, cbv_ref, wd_ref,
                o_ref, hn_ref, acc_ref, *, tm, seq, nf):
    i = pl.program_id(0)
    f = pl.program_id(1)

    @pl.when(f == 0)
    def _():
        hn_ref[FFN_HALO:, :] = _bf(_rms(hm_ref[...], nw_ref[...]))
        keep = ((i * tm) % seq) != 0
        halo = _rms(hh_ref[...], nw_ref[...])
        hn_ref[0:FFN_HALO, :] = _bf(jnp.where(keep, halo, 0.0))

    hn = hn_ref[...]

    def conv(u, cw, cb):
        return (cb + cw[2:3] * u[FFN_HALO:] + cw[1:2] * u[FFN_HALO - 1:FFN_HALO - 1 + tm]
                + cw[0:1] * u[FFN_HALO - 2:FFN_HALO - 2 + tm])

    ug = conv(_dot(hn, wg_ref[...]), cwg_ref[...], cbg_ref[...])
    uv = conv(_dot(hn, wv_ref[...]), cwv_ref[...], cbv_ref[...])
    t = _dot(_bf(_gelu(ug) * uv), wd_ref[...])

    @pl.when(f == 0)
    def _():
        acc_ref[...] = t

    @pl.when(f != 0)
    def _():
        acc_ref[...] += t

    @pl.when(f == nf - 1)
    def _():
        o_ref[...] = hm_ref[...] + acc_ref[...]


def _ffn(h, nw, wup, cw, cb, wd, seq, tm, tf):
    T, D = h.shape
    nf = D_FF // tf
    hb = tm // FFN_HALO
    return pl.pallas_call(
        functools.partial(_ffn_kernel, tm=tm, seq=seq, nf=nf),
        grid=(T // tm, nf),
        in_specs=[
            pl.BlockSpec((tm, D), lambda i, f: (i, 0)),
            pl.BlockSpec((FFN_HALO, D), lambda i, f: (jnp.maximum(i * hb - 1, 0), 0)),
            pl.BlockSpec((1, D), lambda i, f: (0, 0)),
            pl.BlockSpec((D, tf), lambda i, f: (0, f)),
            pl.BlockSpec((D, tf), lambda i, f: (0, f + nf)),
            pl.BlockSpec((3, tf), lambda i, f: (0, f)),
            pl.BlockSpec((1, tf), lambda i, f: (0, f)),
            pl.BlockSpec((3, tf), lambda i, f: (0, f + nf)),
            pl.BlockSpec((1, tf), lambda i, f: (0, f + nf)),
            pl.BlockSpec((tf, D), lambda i, f: (f, 0)),
        ],
        out_specs=pl.BlockSpec((tm, D), lambda i, f: (i, 0)),
        out_shape=jax.ShapeDtypeStruct((T, D), F32),
        scratch_shapes=[pltpu.VMEM((tm + FFN_HALO, D), BF16), pltpu.VMEM((tm, D), F32)],
        compiler_params=_params(("parallel", "arbitrary")),
        name="ffn",
    )(h, h, nw, wup, wup, cw, cb, cw, cb, wd)


def _ple_kernel(h_ref, p_ref, nw_ref, wg_ref, wp_ref, o_ref):
    h = h_ref[...]
    gate = _sigmoid(_dot(_bf(_rms(h, nw_ref[...])), wg_ref[...]))
    o_ref[...] = h + gate * _dot(_bf(p_ref[...]), wp_ref[...])


def _ple(h, p, nw, wg, wp, tm):
    T, D = h.shape
    Q = p.shape[1]
    return pl.pallas_call(
        _ple_kernel,
        grid=(T // tm,),
        in_specs=[
            pl.BlockSpec((tm, D), lambda i: (i, 0)),
            pl.BlockSpec((tm, Q), lambda i: (i, 0)),
            pl.BlockSpec((1, D), lambda i: (0, 0)),
            pl.BlockSpec((D, D), lambda i: (0, 0)),
            pl.BlockSpec((Q, D), lambda i: (0, 0)),
        ],
        out_specs=pl.BlockSpec((tm, D), lambda i: (i, 0)),
        out_shape=jax.ShapeDtypeStruct((T, D), F32),
        compiler_params=_params(("parallel",)),
        name="ple",
    )(h, p, nw, wg, wp)


def _pack_w_in(w):
    return _bf(jnp.pad(w, ((0, 0), (0, PROJ_COLS - IN_COLS))))


def _block_diag(w):
    nb, n, _ = w.shape
    eye = jnp.eye(nb, dtype=w.dtype)
    return (eye[:, None, :, None] * w[:, :, None, :]).reshape(nb * n, nb * n)


def _pad_row(v, n):
    return jnp.zeros((1, n), F32).at[0, :v.shape[0]].set(v)


def _pick(n, pref):
    return pref if n % pref == 0 else n


def kernel(x, p, mix_norm, w_in, lru_conv_w, lru_conv_b, lru_wr, lru_br, lru_wi, lru_bi, lru_lambda, fox_fbias, fox_q_gain, fox_k_gain, ssd_conv_w, ssd_conv_b, ssd_dt_bias, ssd_a_log, ssd_d, ssd_norm, rwkv_mu, rwkv_w0, rwkv_w2, rwkv_a0, rwkv_a2, rwkv_g2, rwkv_kk, rwkv_ka, rwkv_rk, rwkv_ln_w, rwkv_ln_b, w_branch, w_out, ffn_norm, w_up, ffn_conv_w, ffn_conv_b, w_down, ple_norm, w_ple_gate, w_ple):
    B, S, D = x.shape
    T = B * S
    h = x.reshape(T, D)
    seg = _bf(_block_diag(jnp.ones((RWKV_HEADS, RWKV_HEAD, RWKV_HEAD), F32)))
    r1 = lambda v: v.reshape(1, -1)
    for i in range(DEPTH):
        proj = _norm_matmul(h, r1(mix_norm[i]), _pack_w_in(w_in[i]), _pick(T, 1024), 1024)
        proj3 = proj.reshape(B, S, PROJ_COLS)
        y_a = _lru(proj3, lru_conv_w[i], r1(lru_conv_b[i]), _bf(_block_diag(lru_wr[i])), r1(lru_br[i]),
                   _bf(_block_diag(lru_wi[i])), r1(lru_bi[i]), r1(lru_lambda[i]), _pick(S, 256))
        qn, kn, vb, ct = _fox_prep(proj3, r1(fox_q_gain[i]), r1(fox_k_gain[i]),
                                   _pad_row(fox_fbias[i], 128), _pick(S, 512))
        y_b = _fox_attn(qn, kn, vb, ct, _pick(S, 512))
        y_c = _ssd(proj3, ssd_conv_w[i], r1(ssd_conv_b[i]), _pad_row(ssd_dt_bias[i], 128),
                   _pad_row(ssd_a_log[i], 128), r1(jnp.repeat(ssd_d[i], SSD_HEAD_DIM)),
                   r1(ssd_norm[i]), _pick(S, 256))
        y_d = _rwkv(proj3, r1(rwkv_mu[i]), r1(rwkv_w0[i]), _bf(rwkv_w2[i]), r1(rwkv_a0[i]),
                    _bf(rwkv_a2[i]), _bf(rwkv_g2[i]), r1(rwkv_kk[i]), r1(rwkv_ka[i]),
                    r1(rwkv_rk[i]), r1(rwkv_ln_w[i]), r1(rwkv_ln_b[i]), seg, 64)
        ys = [y.reshape(T, BRANCH_WIDTH) for y in (y_a, y_b, y_c, y_d)]
        h = _merge(h, proj, ys, _bf(w_branch[i]), _bf(w_out[i]), _pick(T, 256))
        h = _ffn(h, r1(ffn_norm[i]), _bf(w_up[i]), ffn_conv_w[i], r1(ffn_conv_b[i]), _bf(w_down[i]),
                 S, _pick(S, 512), 512)
        h = _ple(h, p[i].reshape(T, PLE_DIM), r1(ple_norm[i]), _bf(w_ple_gate[i]), _bf(w_ple[i]),
                 _pick(T, 512))
    return h.reshape(B, S, D)
```

```python
import functools

import jax
import jax.numpy as jnp
from jax import lax
from jax.experimental import pallas as pl
from jax.experimental.pallas import tpu as pltpu

F32 = jnp.float32
BF16 = jnp.bfloat16

D_MODEL = 2048
DEPTH = 4
N_BRANCH = 4
BRANCH_WIDTH = 512
LRU_C = 8.0
FOX_HEADS = 4
FOX_HEAD_DIM = 128
SSD_HEADS = 8
SSD_HEAD_DIM = 64
SSD_GROUPS = 2
SSD_STATE = 128
RWKV_HEADS = 8
RWKV_HEAD = 64
RWKV_LN_EPS = 64e-5
D_FF = 5632
PLE_DIM = 256
EPS = 1e-6

IN_COLS = 14092
PROJ_COLS = 14336
COL_GATES = 0
COL_AX, COL_AG = 8192, 8704
COL_BQ, COL_BK, COL_BV = 9216, 9728, 10240
COL_BF = 10752
COL_C = 10752
COL_D = 12288

VMEM_LIMIT = 56 * 1024 * 1024
NEG_BIG = -1e30


def _dot(a, b):
    return jnp.dot(a, b, preferred_element_type=F32)


def _dot_nt(a, b):
    return lax.dot_general(a, b, (((1,), (1,)), ((), ())), preferred_element_type=F32)


def _dot_tn(a, b):
    return lax.dot_general(a, b, (((0,), (0,)), ((), ())), preferred_element_type=F32)


def _bf(x):
    return x.astype(BF16)


def _split3(x):
    x1 = x.astype(BF16)
    r1 = x - x1.astype(F32)
    x2 = r1.astype(BF16)
    x3 = (r1 - x2.astype(F32)).astype(BF16)
    return x1, x2, x3


def _dot_exact_lhs(m, x):
    x1, x2, x3 = _split3(x)
    return _dot(m, x1) + _dot(m, x2) + _dot(m, x3)


def _dot_exact_rhs(x, m):
    x1, x2, x3 = _split3(x)
    return _dot(x1, m) + _dot(x2, m) + _dot(x3, m)


def _softplus(x):
    return jnp.maximum(x, 0.0) + jnp.log1p(jnp.exp(-jnp.abs(x)))


def _sigmoid(x):
    return jax.nn.sigmoid(x)


def _gelu(x):
    return jax.nn.gelu(x)


def _tril_ones(n, dtype=BF16):
    r = lax.broadcasted_iota(jnp.int32, (n, n), 0)
    c = lax.broadcasted_iota(jnp.int32, (n, n), 1)
    return (c <= r).astype(dtype)


def _rms(x, w):
    ms = jnp.mean(x * x, axis=-1, keepdims=True)
    return x * lax.rsqrt(ms + EPS) * w


def _params(sem):
    return pltpu.CompilerParams(dimension_semantics=sem, vmem_limit_bytes=VMEM_LIMIT)


def _norm_matmul_kernel(h_ref, nw_ref, w_ref, o_ref, xn_ref):
    @pl.when(pl.program_id(1) == 0)
    def _():
        xn_ref[...] = _bf(_rms(h_ref[...], nw_ref[...]))

    o_ref[...] = _dot(xn_ref[...], w_ref[...])


def _norm_matmul(h, nw, w, tm, tn):
    T, D = h.shape
    N = w.shape[1]
    return pl.pallas_call(
        _norm_matmul_kernel,
        grid=(T // tm, N // tn),
        in_specs=[
            pl.BlockSpec((tm, D), lambda i, j: (i, 0)),
            pl.BlockSpec((1, D), lambda i, j: (0, 0)),
            pl.BlockSpec((D, tn), lambda i, j: (0, j)),
        ],
        out_specs=pl.BlockSpec((tm, tn), lambda i, j: (i, j)),
        out_shape=jax.ShapeDtypeStruct((T, N), F32),
        scratch_shapes=[pltpu.VMEM((tm, D), BF16)],
        compiler_params=_params(("parallel", "arbitrary")),
        name="in_proj",
    )(h, nw, w)


def _causal_conv(x, halo, cw, cb, width):
    n = x.shape[0]
    xe = jnp.concatenate([halo, x], axis=0)
    y = cb + cw[width - 1:width] * x
    for k in range(width - 1):
        s = width - 1 - k
        y = y + cw[k:k + 1] * xe[8 - s:8 - s + n]
    return y


def _lru_kernel(x_ref, g_ref, cw_ref, cb_ref, wr_ref, br_ref, wi_ref, bi_ref, lam_ref,
                o_ref, halo_ref, st_ref, *, tc):
    @pl.when(pl.program_id(1) == 0)
    def _():
        halo_ref[...] = jnp.zeros_like(halo_ref)
        st_ref[...] = jnp.zeros_like(st_ref)

    x = x_ref[0]
    xc = _causal_conv(x, halo_ref[...], cw_ref[...], cb_ref[...], 4)
    halo_ref[...] = x[tc - 8:]
    xb = _bf(xc)
    r = _sigmoid(_dot(xb, wr_ref[...]) + br_ref[...])
    gi = _sigmoid(_dot(xb, wi_ref[...]) + bi_ref[...])
    log_a = -LRU_C * r * _softplus(-lam_ref[...])
    a = jnp.exp(log_a)
    u = xc * gi * jnp.sqrt(-jnp.tanh(log_a) * (1.0 + a * a))
    row = lax.broadcasted_iota(jnp.int32, (tc, 1), 0)
    u = u + jnp.where(row == 0, a * st_ref[0:1], 0.0)
    d = 1
    while d < tc:
        keep = row >= d
        u_sh = jnp.where(keep, pltpu.roll(u, d, 0), 0.0)
        a_sh = jnp.where(keep, pltpu.roll(a, d, 0), 1.0)
        u = a * u_sh + u
        a = a * a_sh
        d *= 2
    st_ref[0:1] = u[tc - 1:tc]
    o_ref[0] = _bf(u * _gelu(g_ref[0]))


def _lru(proj3, cw, cb, wr, br, wi, bi, lam, tc):
    B, S, _ = proj3.shape
    W = BRANCH_WIDTH
    row = lambda b, c: (0, 0)
    return pl.pallas_call(
        functools.partial(_lru_kernel, tc=tc),
        grid=(B, S // tc),
        in_specs=[
            pl.BlockSpec((1, tc, W), lambda b, c: (b, c, COL_AX // W)),
            pl.BlockSpec((1, tc, W), lambda b, c: (b, c, COL_AG // W)),
            pl.BlockSpec((4, W), row),
            pl.BlockSpec((1, W), row),
            pl.BlockSpec((W, W), row),
            pl.BlockSpec((1, W), row),
            pl.BlockSpec((W, W), row),
            pl.BlockSpec((1, W), row),
            pl.BlockSpec((1, W), row),
        ],
        out_specs=pl.BlockSpec((1, tc, W), lambda b, c: (b, c, 0)),
        out_shape=jax.ShapeDtypeStruct((B, S, W), BF16),
        scratch_shapes=[pltpu.VMEM((8, W), F32), pltpu.VMEM((8, W), F32)],
        compiler_params=_params(("parallel", "arbitrary")),
        name="rg_lru",
    )(proj3, proj3, cw, cb, wr, br, wi, bi, lam)


def _fox_prep_kernel(q_ref, k_ref, v_ref, f_ref, qg_ref, kg_ref, fb_ref,
                     qo_ref, ko_ref, vo_ref, ct_ref, off_ref, *, tc):
    @pl.when(pl.program_id(1) == 0)
    def _():
        off_ref[...] = jnp.zeros_like(off_ref)

    scale = FOX_HEAD_DIM ** -0.5
    for h in range(FOX_HEADS):
        sl = slice(h * FOX_HEAD_DIM, (h + 1) * FOX_HEAD_DIM)
        qo_ref[0, :, sl] = _bf(_rms(q_ref[0, :, sl], qg_ref[...]) * scale)
        ko_ref[0, :, sl] = _bf(_rms(k_ref[0, :, sl], kg_ref[...]))
    vo_ref[0] = _bf(v_ref[0])
    log_f = -_softplus(-(f_ref[0] + fb_ref[...]))
    cs = _dot_exact_lhs(_tril_ones(tc), log_f) + off_ref[0:1]
    off_ref[0:1] = cs[tc - 1:tc]
    ct_ref[0] = cs.T[0:8]


def _fox_prep(proj3, qg, kg, fb, tc):
    B, S, _ = proj3.shape
    W = BRANCH_WIDTH
    row = lambda b, c: (0, 0)
    return pl.pallas_call(
        functools.partial(_fox_prep_kernel, tc=tc),
        grid=(B, S // tc),
        in_specs=[
            pl.BlockSpec((1, tc, W), lambda b, c: (b, c, COL_BQ // W)),
            pl.BlockSpec((1, tc, W), lambda b, c: (b, c, COL_BK // W)),
            pl.BlockSpec((1, tc, W), lambda b, c: (b, c, COL_BV // W)),
            pl.BlockSpec((1, tc, 128), lambda b, c: (b, c, COL_BF // 128)),
            pl.BlockSpec((1, FOX_HEAD_DIM), row),
            pl.BlockSpec((1, FOX_HEAD_DIM), row),
            pl.BlockSpec((1, 128), row),
        ],
        out_specs=[
            pl.BlockSpec((1, tc, W), lambda b, c: (b, c, 0)),
            pl.BlockSpec((1, tc, W), lambda b, c: (b, c, 0)),
            pl.BlockSpec((1, tc, W), lambda b, c: (b, c, 0)),
            pl.BlockSpec((1, 8, tc), lambda b, c: (b, 0, c)),
        ],
        out_shape=[
            jax.ShapeDtypeStruct((B, S, W), BF16),
            jax.ShapeDtypeStruct((B, S, W), BF16),
            jax.ShapeDtypeStruct((B, S, W), BF16),
            jax.ShapeDtypeStruct((B, 8, S), F32),
        ],
        scratch_shapes=[pltpu.VMEM((8, 128), F32)],
        compiler_params=_params(("parallel", "arbitrary")),
        name="fox_prep",
    )(proj3, proj3, proj3, proj3, qg, kg, fb)


def _fox_attn_kernel(q_ref, k_ref, v_ref, ck_ref, o_ref, *, tq):
    h = pl.program_id(1)
    qi = pl.program_id(2)
    q = q_ref[0]

    def scores(ks):
        k = k_ref[0, pl.ds(ks, tq), :]
        ck = ck_ref[0, pl.ds(h, 1), pl.ds(ks, tq)]
        return _dot_nt(q, k) - ck

    def update(carry, s, ks):
        m, l, acc = carry
        m_new = jnp.maximum(m, jnp.max(s, axis=1, keepdims=True))
        p = jnp.exp(s - m_new)
        alpha = jnp.exp(m - m_new)
        l = alpha * l + jnp.sum(p, axis=1, keepdims=True)
        acc = alpha * acc + _dot(_bf(p), v_ref[0, pl.ds(ks, tq), :])
        return m_new, l, acc

    def pair(j, carry):
        ks0 = pl.multiple_of(2 * j * tq, tq)
        ks1 = pl.multiple_of((2 * j + 1) * tq, tq)
        s0 = scores(ks0)
        s1 = scores(ks1)
        return update(update(carry, s0, ks0), s1, ks1)

    init = (jnp.full((tq, 1), NEG_BIG, F32), jnp.zeros((tq, 1), F32),
            jnp.zeros((tq, FOX_HEAD_DIM), F32))
    carry = lax.fori_loop(0, qi // 2, pair, init)
    ks = pl.multiple_of(qi * tq, tq)
    r = lax.broadcasted_iota(jnp.int32, (tq, tq), 0)
    c = lax.broadcasted_iota(jnp.int32, (tq, tq), 1)
    s_diag = jnp.where(c <= r, scores(ks), NEG_BIG)

    def odd(carry):
        kp = pl.multiple_of((qi - 1) * tq, tq)
        return update(carry, scores(kp), kp)

    carry = lax.cond(qi % 2 == 1, odd, lambda carry: carry, carry)
    _, l, acc = update(carry, s_diag, ks)
    o_ref[0] = _bf(acc / l)


def _fox_attn(qn, kn, vb, ct, tq):
    B, S, W = qn.shape
    Dh = FOX_HEAD_DIM
    return pl.pallas_call(
        functools.partial(_fox_attn_kernel, tq=tq),
        grid=(B, FOX_HEADS, S // tq),
        in_specs=[
            pl.BlockSpec((1, tq, Dh), lambda b, h, i: (b, i, h)),
            pl.BlockSpec((1, S, Dh), lambda b, h, i: (b, 0, h)),
            pl.BlockSpec((1, S, Dh), lambda b, h, i: (b, 0, h)),
            pl.BlockSpec((1, 8, S), lambda b, h, i: (b, 0, 0)),
        ],
        out_specs=pl.BlockSpec((1, tq, Dh), lambda b, h, i: (b, i, h)),
        out_shape=jax.ShapeDtypeStruct((B, S, W), BF16),
        compiler_params=_params(("parallel", "parallel", "arbitrary")),
        name="fox_attn",
    )(qn, kn, vb, ct)


SSD_BLOCK = 1792
SSD_SHIFT = 4


def _ssd_kernel(u_ref, cw_ref, cb_ref, dtb_ref, alog_ref, dsk_ref, nw_ref,
                o_ref, halo_ref, st_ref, *, lc):
    @pl.when(pl.program_id(1) == 0)
    def _():
        halo_ref[...] = jnp.zeros_like(halo_ref)
        st_ref[...] = jnp.zeros_like(st_ref)

    P, N = SSD_HEAD_DIM, SSD_STATE
    W = SSD_HEADS * P
    u = pltpu.roll(u_ref[0], SSD_BLOCK - SSD_SHIFT, 1)
    z = u[:, 0:W]
    x = u[:, W:W + 1024]
    dt_raw = u[:, W + 1024:W + 1152]
    xc = _causal_conv(x, halo_ref[...], cw_ref[...], cb_ref[...], 4)
    halo_ref[...] = x[lc - 8:]
    xc = xc * _sigmoid(xc)
    xs = xc[:, :W]
    dt = _softplus(dt_raw + dtb_ref[...])
    a_neg = -jnp.exp(alog_ref[...])
    cum = _dot_exact_lhs(_tril_ones(lc), dt * a_neg)
    cum_t = cum.T
    cum_last = cum[lc - 1:lc]
    r = lax.broadcasted_iota(jnp.int32, (lc, lc), 0)
    c = lax.broadcasted_iota(jnp.int32, (lc, lc), 1)
    tril = c <= r
    rep = SSD_HEADS // SSD_GROUPS
    ys = []
    cb_g = [None] * SSD_GROUPS
    for h in range(SSD_HEADS):
        g = h // rep
        bm = _bf(xc[:, W + g * N:W + (g + 1) * N])
        cm = _bf(xc[:, W + SSD_GROUPS * N + g * N:W + SSD_GROUPS * N + (g + 1) * N])
        if cb_g[g] is None:
            cb_g[g] = _dot_nt(cm, bm)
        cum_col = cum[:, h:h + 1]
        dt_col = dt[:, h:h + 1]
        lmat = jnp.where(tril, jnp.exp(cum_col - cum_t[h:h + 1, :]), 0.0)
        xh = xs[:, h * P:(h + 1) * P]
        xdt = xh * dt_col
        y = _dot(_bf(cb_g[g] * lmat), _bf(xdt))
        st = st_ref[h]
        y = y + _dot(cm, _bf(st)) * jnp.exp(cum_col)
        last = cum_last[:, h:h + 1]
        xdec = xdt * jnp.exp(last - cum_col)
        st_ref[h] = jnp.exp(last) * st + _dot_tn(bm, _bf(xdec))
        ys.append(y + xh * dsk_ref[:, h * P:(h + 1) * P])
    y = jnp.concatenate(ys, axis=1)
    y = y * (z * _sigmoid(z))
    gw = W // SSD_GROUPS
    outs = []
    for g in range(SSD_GROUPS):
        outs.append(_rms(y[:, g * gw:(g + 1) * gw], nw_ref[:, g * gw:(g + 1) * gw]))
    o_ref[0] = _bf(jnp.concatenate(outs, axis=1))


def _ssd(proj3, cw, cb, dtb, alog, dsk, nw, lc):
    B, S, _ = proj3.shape
    W = BRANCH_WIDTH
    row = lambda b, c: (0, 0)
    return pl.pallas_call(
        functools.partial(_ssd_kernel, lc=lc),
        grid=(B, S // lc),
        in_specs=[
            pl.BlockSpec((1, lc, SSD_BLOCK), lambda b, c: (b, c, COL_C // SSD_BLOCK)),
            pl.BlockSpec((4, 1024), row),
            pl.BlockSpec((1, 1024), row),
            pl.BlockSpec((1, 128), row),
            pl.BlockSpec((1, 128), row),
            pl.BlockSpec((1, W), row),
            pl.BlockSpec((1, W), row),
        ],
        out_specs=pl.BlockSpec((1, lc, W), lambda b, c: (b, c, 0)),
        out_shape=jax.ShapeDtypeStruct((B, S, W), BF16),
        scratch_shapes=[pltpu.VMEM((8, 1024), F32),
                        pltpu.VMEM((SSD_HEADS, SSD_STATE, SSD_HEAD_DIM), F32)],
        compiler_params=_params(("parallel", "arbitrary")),
        name="ssd",
    )(proj3, cw, cb, dtb, alog, dsk, nw)


RWKV_BLOCK = 2048
RWKV_SHIFT = 12
RWKV_CHUNK = 64


def _rwkv_kernel(u_ref, mu_ref, w0_ref, w2_ref, a0_ref, a2_ref, g2_ref, kkw_ref, kaw_ref, rk_ref,
                 lnw_ref, lnb_ref, seg_ref, o_ref, prev_ref, st_ref, *, lc):
    @pl.when(pl.program_id(1) == 0)
    def _():
        prev_ref[...] = jnp.zeros_like(prev_ref)
        st_ref[...] = jnp.zeros_like(st_ref)

    L = RWKV_CHUNK
    C = lc // L
    N = RWKV_HEAD
    HEADS = range(RWKV_HEADS)
    W = RWKV_HEADS * N
    row = lax.broadcasted_iota(jnp.int32, (lc, 1), 0)

    x = pltpu.roll(u_ref[0], RWKV_BLOCK - RWKV_SHIFT, 1)[:, :3 * W + 256]
    sh = jnp.where(row == 0, prev_ref[0:1], pltpu.roll(x, 1, 0))
    prev_ref[0:1] = x[lc - 1:lc]
    u = x + (sh - x) * mu_ref[...]
    r, k, v = u[:, 0:W], u[:, W:2 * W], u[:, 2 * W:3 * W]
    wl, al, gl = u[:, 3 * W:3 * W + 64], u[:, 3 * W + 64:3 * W + 128], u[:, 3 * W + 128:3 * W + 256]

    seg = seg_ref[...]

    def seg_sum(x):
        x1 = _bf(x)
        return _dot(x1, seg) + _dot(_bf(x - x1.astype(F32)), seg)

    w = -_softplus(-(w0_ref[...] + _dot(_bf(jnp.tanh(wl)), w2_ref[...]))) - 0.5
    ld = -jnp.exp(w)
    a = _sigmoid(a0_ref[...] + _dot(_bf(al), a2_ref[...]))
    g = _dot(_bf(_sigmoid(gl)), g2_ref[...])
    kk = k * kkw_ref[...]
    kp = k * (1.0 + (a - 1.0) * kaw_ref[...])
    sums = seg_sum(jnp.concatenate([kk * kk, r * kp * rk_ref[...]], axis=0))
    kk = kk / jnp.maximum(jnp.sqrt(sums[:lc]), 1e-12)
    bonus = sums[lc:]
    kka = kk * a

    bi = lax.broadcasted_iota(jnp.int32, (lc, lc), 0)
    bj = lax.broadcasted_iota(jnp.int32, (lc, lc), 1)
    tril_chunks = ((bj <= bi) & (bi // L == bj // L)).astype(BF16)
    cum = _dot_exact_lhs(tril_chunks, ld)
    rs = [slice(j * L, (j + 1) * L) for j in range(C)]
    c_last = [cum[(j + 1) * L - 1:(j + 1) * L] for j in range(C)]
    c_last_rows = jnp.concatenate([jnp.broadcast_to(c_last[j], (L, W)) for j in range(C)], axis=0)
    e_neg = jnp.exp(-cum)
    e_end = jnp.exp(c_last_rows - cum)
    at = _bf(-kk * jnp.exp(cum - ld))
    rt = r * jnp.exp(cum)
    rtb = _bf(rt)
    bt = _bf(kka * e_neg)
    kt = _bf(kp * e_neg)
    bp = _bf(kka * e_end)
    kpe = _bf(kp * e_end)
    w_end = [jnp.exp(c_last[j]) for j in range(C)]
    vb = _bf(v)

    ri = lax.broadcasted_iota(jnp.int32, (L, L), 0)
    ci = lax.broadcasted_iota(jnp.int32, (L, L), 1)
    strict = ci < ri
    incl = ci <= ri
    eye = (ci == ri).astype(F32)
    sl = [slice(h * N, (h + 1) * N) for h in HEADS]
    q = [(rs[j], sl[h]) for j in range(C) for h in HEADS]
    P = range(len(q))

    gm = [_dot_nt(jnp.concatenate([at[q[i]], rtb[q[i]]], axis=0),
                  jnp.concatenate([bt[q[i]], kt[q[i]]], axis=0)) for i in P]
    a_ab = [_bf(jnp.where(strict, gm[i][:L, :L], 0.0)) for i in P]
    a_ak = [_bf(jnp.where(strict, gm[i][:L, L:], 0.0)) for i in P]
    a_rb = [_bf(jnp.where(incl, gm[i][L:, :L], 0.0)) for i in P]
    a_rk = [_bf(jnp.where(incl, gm[i][L:, L:], 0.0)) for i in P]
    pw = [_dot(a_ab[i], a_ab[i]) for i in P]
    tinv = [eye + a_ab[i].astype(F32) for i in P]
    zv = [_dot(a_ak[i], vb[q[i]]) for i in P]
    n_done = 2
    while 2 * n_done < L:
        both = [_dot(_bf(jnp.concatenate([tinv[i], pw[i]], axis=0)), _bf(pw[i])) for i in P]
        tinv = [tinv[i] + both[i][:L] for i in P]
        pw = [both[i][L:] for i in P]
        n_done *= 2
    tinv = [tinv[i] + _dot(_bf(tinv[i]), _bf(pw[i])) for i in P]
    tz = [_dot(_bf(tinv[i]), jnp.concatenate([at[q[i]], _bf(zv[i])], axis=1)) for i in P]
    tzb = [_bf(tz[i]) for i in P]
    mix = [_dot(a_rb[i], tzb[i]) for i in P]
    yv = [mix[i][:, N:] + _dot(a_rk[i], vb[q[i]]) for i in P]
    r2 = [_bf(rt[q[i]] + mix[i][:, :N]) for i in P]
    m_low = [_bf(_dot_tn(tzb[i][:, :N], bp[q[i]])) for i in P]
    c_add = [_dot_tn(jnp.concatenate([tzb[i][:, N:], vb[q[i]]], axis=0),
                     jnp.concatenate([bp[q[i]], kpe[q[i]]], axis=0)) for i in P]
    s = [st_ref[h] for h in HEADS]
    y_rows = []
    for j in range(C):
        sb = [_bf(s[h]) for h in HEADS]
        y_rows.append(jnp.concatenate(
            [_dot_nt(r2[j * RWKV_HEADS + h], sb[h]) + yv[j * RWKV_HEADS + h] for h in HEADS], axis=1))
        s = [s[h] * w_end[j][:, sl[h]] + _dot(sb[h], m_low[j * RWKV_HEADS + h]) + c_add[j * RWKV_HEADS + h]
             for h in HEADS]
    for h in HEADS:
        st_ref[h] = s[h]
    y = jnp.concatenate(y_rows, axis=0)

    inv_n = 1.0 / N
    mean = seg_sum(y) * inv_n
    yc = y - mean
    var = seg_sum(yc * yc) * inv_n
    y = yc * lax.rsqrt(var + RWKV_LN_EPS) * lnw_ref[...] + lnb_ref[...]
    y = y + bonus * v
    o_ref[0] = _bf(y * g)


def _rwkv(proj3, mu, w0, w2, a0, a2, g2, kkw, kaw, rk, lnw, lnb, seg, lc):
    B, S, _ = proj3.shape
    W = BRANCH_WIDTH
    row = lambda b, c: (0, 0)
    vec = pl.BlockSpec((1, W), row)
    return pl.pallas_call(
        functools.partial(_rwkv_kernel, lc=lc),
        grid=(B, S // lc),
        in_specs=[
            pl.BlockSpec((1, lc, RWKV_BLOCK), lambda b, c: (b, c, COL_D // RWKV_BLOCK)),
            pl.BlockSpec((1, 3 * W + 256), row),
            vec, pl.BlockSpec((64, W), row),
            vec, pl.BlockSpec((64, W), row),
            pl.BlockSpec((128, W), row),
            vec, vec, vec, vec, vec,
            pl.BlockSpec((W, W), row),
        ],
        out_specs=pl.BlockSpec((1, lc, W), lambda b, c: (b, c, 0)),
        out_shape=jax.ShapeDtypeStruct((B, S, W), BF16),
        scratch_shapes=[pltpu.VMEM((8, 3 * W + 256), F32),
                        pltpu.VMEM((RWKV_HEADS, RWKV_HEAD, RWKV_HEAD), F32)],
        compiler_params=_params(("parallel", "arbitrary")),
        name="rwkv7",
    )(proj3, mu, w0, w2, a0, a2, g2, kkw, kaw, rk, lnw, lnb, seg)


def _merge_kernel(h_ref, g_ref, ya_ref, yb_ref, yc_ref, yd_ref, wb_ref, wo_ref, o_ref):
    D = D_MODEL
    acc = None
    for n, y_ref in enumerate((ya_ref, yb_ref, yc_ref, yd_ref)):
        t = _sigmoid(g_ref[:, n * D:(n + 1) * D]) * _dot(y_ref[...], wb_ref[n])
        acc = t if acc is None else acc + t
    o_ref[...] = h_ref[...] + _dot(_bf(acc), wo_ref[...])


def _merge(h, proj, ys, wb, wo, tm):
    T, D = h.shape
    W = BRANCH_WIDTH
    yspec = pl.BlockSpec((tm, W), lambda i: (i, 0))
    once = pl.Buffered(1)
    return pl.pallas_call(
        _merge_kernel,
        grid=(T // tm,),
        in_specs=[
            pl.BlockSpec((tm, D), lambda i: (i, 0)),
            pl.BlockSpec((tm, N_BRANCH * D), lambda i: (i, 0)),
            yspec, yspec, yspec, yspec,
            pl.BlockSpec((N_BRANCH, W, D), lambda i: (0, 0, 0), pipeline_mode=once),
            pl.BlockSpec((D, D), lambda i: (0, 0), pipeline_mode=once),
        ],
        out_specs=pl.BlockSpec((tm, D), lambda i: (i, 0)),
        out_shape=jax.ShapeDtypeStruct((T, D), F32),
        compiler_params=_params(("parallel",)),
        name="merge",
    )(h, proj, *ys, wb, wo)


FFN_HALO = 16


def _ffn_kernel(hm_ref, hh_ref, nw_ref, wg_ref, wv_ref, cwg_ref, cbg_ref, cwv_ref, cbv_ref, wd_ref,
                o_ref, hn_ref, acc_ref, *, tm, seq, nf):
    i = pl.program_id(0)
    f = pl.program_id(1)

    @pl.when(f == 0)
    def _():
        hn_ref[FFN_HALO:, :] = _bf(_rms(hm_ref[...], nw_ref[...]))
        keep = ((i * tm) % seq) != 0
        halo = _rms(hh_ref[...], nw_ref[...])
        hn_ref[0:FFN_HALO, :] = _bf(jnp.where(keep, halo, 0.0))

    hn = hn_ref[...]

    def conv(u, cw, cb):
        return (cb + cw[2:3] * u[FFN_HALO:] + cw[1:2] * u[FFN_HALO - 1:FFN_HALO - 1 + tm]
                + cw[0:1] * u[FFN_HALO - 2:FFN_HALO - 2 + tm])

    ug = conv(_dot(hn, wg_ref[...]), cwg_ref[...], cbg_ref[...])
    uv = conv(_dot(hn, wv_ref[...]), cwv_ref[...], cbv_ref[...])
    t = _dot(_bf(_gelu(ug) * uv), wd_ref[...])

    @pl.when(f == 0)
    def _():
        acc_ref[...] = t

    @pl.when(f != 0)
    def _():
        acc_ref[...] += t

    @pl.when(f == nf - 1)
    def _():
        o_ref[...] = hm_ref[...] + acc_ref[...]


def _ffn(h, nw, wup, cw, cb, wd, seq, tm, tf):
    T, D = h.shape
    nf = D_FF // tf
    hb = tm // FFN_HALO
    return pl.pallas_call(
        functools.partial(_ffn_kernel, tm=tm, seq=seq, nf=nf),
        grid=(T // tm, nf),
        in_specs=[
            pl.BlockSpec((tm, D), lambda i, f: (i, 0)),
            pl.BlockSpec((FFN_HALO, D), lambda i, f: (jnp.maximum(i * hb - 1, 0), 0)),
            pl.BlockSpec((1, D), lambda i, f: (0, 0)),
            pl.BlockSpec((D, tf), lambda i, f: (0, f)),
            pl.BlockSpec((D, tf), lambda i, f: (0, f + nf)),
            pl.BlockSpec((3, tf), lambda i, f: (0, f)),
            pl.BlockSpec((1, tf), lambda i, f: (0, f)),
            pl.BlockSpec((3, tf), lambda i, f: (0, f + nf)),
            pl.BlockSpec((1, tf), lambda i, f: (0, f + nf)),
            pl.BlockSpec((tf, D), lambda i, f: (f, 0)),
        ],
        out_specs=pl.BlockSpec((tm, D), lambda i, f: (i, 0)),
        out_shape=jax.ShapeDtypeStruct((T, D), F32),
        scratch_shapes=[pltpu.VMEM((tm + FFN_HALO, D), BF16), pltpu.VMEM((tm, D), F32)],
        compiler_params=_params(("parallel", "arbitrary")),
        name="ffn",
    )(h, h, nw, wup, wup, cw, cb, cw, cb, wd)


def _ple_kernel(h_ref, p_ref, nw_ref, wg_ref, wp_ref, o_ref):
    h = h_ref[...]
    gate = _sigmoid(_dot(_bf(_rms(h, nw_ref[...])), wg_ref[...]))
    o_ref[...] = h + gate * _dot(_bf(p_ref[...]), wp_ref[...])


def _ple(h, p, nw, wg, wp, tm):
    T, D = h.shape
    Q = p.shape[1]
    return pl.pallas_call(
        _ple_kernel,
        grid=(T // tm,),
        in_specs=[
            pl.BlockSpec((tm, D), lambda i: (i, 0)),
            pl.BlockSpec((tm, Q), lambda i: (i, 0)),
            pl.BlockSpec((1, D), lambda i: (0, 0)),
            pl.BlockSpec((D, D), lambda i: (0, 0)),
            pl.BlockSpec((Q, D), lambda i: (0, 0)),
        ],
        out_specs=pl.BlockSpec((tm, D), lambda i: (i, 0)),
        out_shape=jax.ShapeDtypeStruct((T, D), F32),
        compiler_params=_params(("parallel",)),
        name="ple",
    )(h, p, nw, wg, wp)


def _pack_w_in(w):
    return _bf(jnp.pad(w, ((0, 0), (0, PROJ_COLS - IN_COLS))))


def _block_diag(w):
    nb, n, _ = w.shape
    eye = jnp.eye(nb, dtype=w.dtype)
    return (eye[:, None, :, None] * w[:, :, None, :]).reshape(nb * n, nb * n)


def _pad_row(v, n):
    return jnp.zeros((1, n), F32).at[0, :v.shape[0]].set(v)


def _pick(n, pref):
    return pref if n % pref == 0 else n


def kernel(x, p, mix_norm, w_in, lru_conv_w, lru_conv_b, lru_wr, lru_br, lru_wi, lru_bi, lru_lambda, fox_fbias, fox_q_gain, fox_k_gain, ssd_conv_w, ssd_conv_b, ssd_dt_bias, ssd_a_log, ssd_d, ssd_norm, rwkv_mu, rwkv_w0, rwkv_w2, rwkv_a0, rwkv_a2, rwkv_g2, rwkv_kk, rwkv_ka, rwkv_rk, rwkv_ln_w, rwkv_ln_b, w_branch, w_out, ffn_norm, w_up, ffn_conv_w, ffn_conv_b, w_down, ple_norm, w_ple_gate, w_ple):
    B, S, D = x.shape
    T = B * S
    h = x.reshape(T, D)
    seg = _bf(_block_diag(jnp.ones((RWKV_HEADS, RWKV_HEAD, RWKV_HEAD), F32)))
    r1 = lambda v: v.reshape(1, -1)
    for i in range(DEPTH):
        proj = _norm_matmul(h, r1(mix_norm[i]), _pack_w_in(w_in[i]), _pick(T, 1024), 1024)
        proj3 = proj.reshape(B, S, PROJ_COLS)
        y_a = _lru(proj3, lru_conv_w[i], r1(lru_conv_b[i]), _bf(_block_diag(lru_wr[i])), r1(lru_br[i]),
                   _bf(_block_diag(lru_wi[i])), r1(lru_bi[i]), r1(lru_lambda[i]), _pick(S, 256))
        qn, kn, vb, ct = _fox_prep(proj3, r1(fox_q_gain[i]), r1(fox_k_gain[i]),
                                   _pad_row(fox_fbias[i], 128), _pick(S, 512))
        y_b = _fox_attn(qn, kn, vb, ct, _pick(S, 512))
        y_c = _ssd(proj3, ssd_conv_w[i], r1(ssd_conv_b[i]), _pad_row(ssd_dt_bias[i], 128),
                   _pad_row(ssd_a_log[i], 128), r1(jnp.repeat(ssd_d[i], SSD_HEAD_DIM)),
                   r1(ssd_norm[i]), _pick(S, 256))
        y_d = _rwkv(proj3, r1(rwkv_mu[i]), r1(rwkv_w0[i]), _bf(rwkv_w2[i]), r1(rwkv_a0[i]),
                    _bf(rwkv_a2[i]), _bf(rwkv_g2[i]), r1(rwkv_kk[i]), r1(rwkv_ka[i]),
                    r1(rwkv_rk[i]), r1(rwkv_ln_w[i]), r1(rwkv_ln_b[i]), seg, _pick(S, 256))
        ys = [y.reshape(T, BRANCH_WIDTH) for y in (y_a, y_b, y_c, y_d)]
        h = _merge(h, proj, ys, _bf(w_branch[i]), _bf(w_out[i]), _pick(T, 256))
        h = _ffn(h, r1(ffn_norm[i]), _bf(w_up[i]), ffn_conv_w[i], r1(ffn_conv_b[i]), _bf(w_down[i]),
                 S, _pick(S, 512), 512)
        h = _ple(h, p[i].reshape(T, PLE_DIM), r1(ple_norm[i]), _bf(w_ple_gate[i]), _bf(w_ple[i]),
                 _pick(T, 512))
    return h.reshape(B, S, D)
```

```python
import functools

import jax
import jax.numpy as jnp
from jax import lax
from jax.experimental import pallas as pl
from jax.experimental.pallas import tpu as pltpu

F32 = jnp.float32
BF16 = jnp.bfloat16

D_MODEL = 2048
DEPTH = 4
N_BRANCH = 4
BRANCH_WIDTH = 512
LRU_C = 8.0
FOX_HEADS = 4
FOX_HEAD_DIM = 128
SSD_HEADS = 8
SSD_HEAD_DIM = 64
SSD_GROUPS = 2
SSD_STATE = 128
RWKV_HEADS = 8
RWKV_HEAD = 64
RWKV_LN_EPS = 64e-5
D_FF = 5632
PLE_DIM = 256
EPS = 1e-6

IN_COLS = 14092
PROJ_COLS = 14336
COL_GATES = 0
COL_AX, COL_AG = 8192, 8704
COL_BQ, COL_BK, COL_BV = 9216, 9728, 10240
COL_BF = 10752
COL_C = 10752
COL_D = 12288

VMEM_LIMIT = 56 * 1024 * 1024
NEG_BIG = -1e30


def _dot(a, b):
    return jnp.dot(a, b, preferred_element_type=F32)


def _dot_nt(a, b):
    return lax.dot_general(a, b, (((1,), (1,)), ((), ())), preferred_element_type=F32)


def _dot_tn(a, b):
    return lax.dot_general(a, b, (((0,), (0,)), ((), ())), preferred_element_type=F32)


def _bf(x):
    return x.astype(BF16)


def _split3(x):
    x1 = x.astype(BF16)
    r1 = x - x1.astype(F32)
    x2 = r1.astype(BF16)
    x3 = (r1 - x2.astype(F32)).astype(BF16)
    return x1, x2, x3


def _dot_exact_lhs(m, x):
    x1, x2, x3 = _split3(x)
    return _dot(m, x1) + _dot(m, x2) + _dot(m, x3)


def _dot_exact_rhs(x, m):
    x1, x2, x3 = _split3(x)
    return _dot(x1, m) + _dot(x2, m) + _dot(x3, m)


def _softplus(x):
    return jnp.maximum(x, 0.0) + jnp.log1p(jnp.exp(-jnp.abs(x)))


def _sigmoid(x):
    return jax.nn.sigmoid(x)


def _gelu(x):
    return jax.nn.gelu(x)


def _tril_ones(n, dtype=BF16):
    r = lax.broadcasted_iota(jnp.int32, (n, n), 0)
    c = lax.broadcasted_iota(jnp.int32, (n, n), 1)
    return (c <= r).astype(dtype)


def _rms(x, w):
    ms = jnp.mean(x * x, axis=-1, keepdims=True)
    return x * lax.rsqrt(ms + EPS) * w


def _params(sem):
    return pltpu.CompilerParams(dimension_semantics=sem, vmem_limit_bytes=VMEM_LIMIT)


def _norm_matmul_kernel(h_ref, nw_ref, w_ref, o_ref, xn_ref):
    @pl.when(pl.program_id(1) == 0)
    def _():
        xn_ref[...] = _bf(_rms(h_ref[...], nw_ref[...]))

    o_ref[...] = _dot(xn_ref[...], w_ref[...])


def _norm_matmul(h, nw, w, tm, tn):
    T, D = h.shape
    N = w.shape[1]
    return pl.pallas_call(
        _norm_matmul_kernel,
        grid=(T // tm, N // tn),
        in_specs=[
            pl.BlockSpec((tm, D), lambda i, j: (i, 0)),
            pl.BlockSpec((1, D), lambda i, j: (0, 0)),
            pl.BlockSpec((D, tn), lambda i, j: (0, j)),
        ],
        out_specs=pl.BlockSpec((tm, tn), lambda i, j: (i, j)),
        out_shape=jax.ShapeDtypeStruct((T, N), F32),
        scratch_shapes=[pltpu.VMEM((tm, D), BF16)],
        compiler_params=_params(("parallel", "arbitrary")),
        name="in_proj",
    )(h, nw, w)


def _causal_conv(x, halo, cw, cb, width):
    n = x.shape[0]
    xe = jnp.concatenate([halo, x], axis=0)
    y = cb + cw[width - 1:width] * x
    for k in range(width - 1):
        s = width - 1 - k
        y = y + cw[k:k + 1] * xe[8 - s:8 - s + n]
    return y


def _lru_kernel(x_ref, g_ref, cw_ref, cb_ref, wr_ref, br_ref, wi_ref, bi_ref, lam_ref,
                o_ref, halo_ref, st_ref, *, tc):
    @pl.when(pl.program_id(1) == 0)
    def _():
        halo_ref[...] = jnp.zeros_like(halo_ref)
        st_ref[...] = jnp.zeros_like(st_ref)

    x = x_ref[0]
    xc = _causal_conv(x, halo_ref[...], cw_ref[...], cb_ref[...], 4)
    halo_ref[...] = x[tc - 8:]
    xb = _bf(xc)
    r = _sigmoid(_dot(xb, wr_ref[...]) + br_ref[...])
    gi = _sigmoid(_dot(xb, wi_ref[...]) + bi_ref[...])
    log_a = -LRU_C * r * _softplus(-lam_ref[...])
    a = jnp.exp(log_a)
    u = xc * gi * jnp.sqrt(-jnp.tanh(log_a) * (1.0 + a * a))
    row = lax.broadcasted_iota(jnp.int32, (tc, 1), 0)
    u = u + jnp.where(row == 0, a * st_ref[0:1], 0.0)
    d = 1
    while d < tc:
        keep = row >= d
        u_sh = jnp.where(keep, pltpu.roll(u, d, 0), 0.0)
        a_sh = jnp.where(keep, pltpu.roll(a, d, 0), 1.0)
        u = a * u_sh + u
        a = a * a_sh
        d *= 2
    st_ref[0:1] = u[tc - 1:tc]
    o_ref[0] = _bf(u * _gelu(g_ref[0]))


def _lru(proj3, cw, cb, wr, br, wi, bi, lam, tc):
    B, S, _ = proj3.shape
    W = BRANCH_WIDTH
    row = lambda b, c: (0, 0)
    return pl.pallas_call(
        functools.partial(_lru_kernel, tc=tc),
        grid=(B, S // tc),
        in_specs=[
            pl.BlockSpec((1, tc, W), lambda b, c: (b, c, COL_AX // W)),
            pl.BlockSpec((1, tc, W), lambda b, c: (b, c, COL_AG // W)),
            pl.BlockSpec((4, W), row),
            pl.BlockSpec((1, W), row),
            pl.BlockSpec((W, W), row),
            pl.BlockSpec((1, W), row),
            pl.BlockSpec((W, W), row),
            pl.BlockSpec((1, W), row),
            pl.BlockSpec((1, W), row),
        ],
        out_specs=pl.BlockSpec((1, tc, W), lambda b, c: (b, c, 0)),
        out_shape=jax.ShapeDtypeStruct((B, S, W), BF16),
        scratch_shapes=[pltpu.VMEM((8, W), F32), pltpu.VMEM((8, W), F32)],
        compiler_params=_params(("parallel", "arbitrary")),
        name="rg_lru",
    )(proj3, proj3, cw, cb, wr, br, wi, bi, lam)


def _fox_prep_kernel(q_ref, k_ref, v_ref, f_ref, qg_ref, kg_ref, fb_ref,
                     qo_ref, ko_ref, vo_ref, ct_ref, off_ref, *, tc):
    @pl.when(pl.program_id(1) == 0)
    def _():
        off_ref[...] = jnp.zeros_like(off_ref)

    scale = FOX_HEAD_DIM ** -0.5
    for h in range(FOX_HEADS):
        sl = slice(h * FOX_HEAD_DIM, (h + 1) * FOX_HEAD_DIM)
        qo_ref[0, :, sl] = _bf(_rms(q_ref[0, :, sl], qg_ref[...]) * scale)
        ko_ref[0, :, sl] = _bf(_rms(k_ref[0, :, sl], kg_ref[...]))
    vo_ref[0] = _bf(v_ref[0])
    log_f = -_softplus(-(f_ref[0] + fb_ref[...]))
    cs = _dot_exact_lhs(_tril_ones(tc), log_f) + off_ref[0:1]
    off_ref[0:1] = cs[tc - 1:tc]
    ct_ref[0] = cs.T[0:8]


def _fox_prep(proj3, qg, kg, fb, tc):
    B, S, _ = proj3.shape
    W = BRANCH_WIDTH
    row = lambda b, c: (0, 0)
    return pl.pallas_call(
        functools.partial(_fox_prep_kernel, tc=tc),
        grid=(B, S // tc),
        in_specs=[
            pl.BlockSpec((1, tc, W), lambda b, c: (b, c, COL_BQ // W)),
            pl.BlockSpec((1, tc, W), lambda b, c: (b, c, COL_BK // W)),
            pl.BlockSpec((1, tc, W), lambda b, c: (b, c, COL_BV // W)),
            pl.BlockSpec((1, tc, 128), lambda b, c: (b, c, COL_BF // 128)),
            pl.BlockSpec((1, FOX_HEAD_DIM), row),
            pl.BlockSpec((1, FOX_HEAD_DIM), row),
            pl.BlockSpec((1, 128), row),
        ],
        out_specs=[
            pl.BlockSpec((1, tc, W), lambda b, c: (b, c, 0)),
            pl.BlockSpec((1, tc, W), lambda b, c: (b, c, 0)),
            pl.BlockSpec((1, tc, W), lambda b, c: (b, c, 0)),
            pl.BlockSpec((1, 8, tc), lambda b, c: (b, 0, c)),
        ],
        out_shape=[
            jax.ShapeDtypeStruct((B, S, W), BF16),
            jax.ShapeDtypeStruct((B, S, W), BF16),
            jax.ShapeDtypeStruct((B, S, W), BF16),
            jax.ShapeDtypeStruct((B, 8, S), F32),
        ],
        scratch_shapes=[pltpu.VMEM((8, 128), F32)],
        compiler_params=_params(("parallel", "arbitrary")),
        name="fox_prep",
    )(proj3, proj3, proj3, proj3, qg, kg, fb)


def _fox_attn_kernel(q_ref, k_ref, v_ref, ck_ref, o_ref, *, tq):
    h = pl.program_id(1)
    qi = pl.program_id(2)
    q = q_ref[0]

    def scores(ks):
        k = k_ref[0, pl.ds(ks, tq), :]
        ck = ck_ref[0, pl.ds(h, 1), pl.ds(ks, tq)]
        return _dot_nt(q, k) - ck

    def update(carry, s, ks):
        m, l, acc = carry
        m_new = jnp.maximum(m, jnp.max(s, axis=1, keepdims=True))
        p = jnp.exp(s - m_new)
        alpha = jnp.exp(m - m_new)
        l = alpha * l + jnp.sum(p, axis=1, keepdims=True)
        acc = alpha * acc + _dot(_bf(p), v_ref[0, pl.ds(ks, tq), :])
        return m_new, l, acc

    def pair(j, carry):
        ks0 = pl.multiple_of(2 * j * tq, tq)
        ks1 = pl.multiple_of((2 * j + 1) * tq, tq)
        s0 = scores(ks0)
        s1 = scores(ks1)
        return update(update(carry, s0, ks0), s1, ks1)

    init = (jnp.full((tq, 1), NEG_BIG, F32), jnp.zeros((tq, 1), F32),
            jnp.zeros((tq, FOX_HEAD_DIM), F32))
    carry = lax.fori_loop(0, qi // 2, pair, init)
    ks = pl.multiple_of(qi * tq, tq)
    r = lax.broadcasted_iota(jnp.int32, (tq, tq), 0)
    c = lax.broadcasted_iota(jnp.int32, (tq, tq), 1)
    s_diag = jnp.where(c <= r, scores(ks), NEG_BIG)

    def odd(carry):
        kp = pl.multiple_of((qi - 1) * tq, tq)
        return update(carry, scores(kp), kp)

    carry = lax.cond(qi % 2 == 1, odd, lambda carry: carry, carry)
    _, l, acc = update(carry, s_diag, ks)
    o_ref[0] = _bf(acc / l)


def _fox_attn(qn, kn, vb, ct, tq):
    B, S, W = qn.shape
    Dh = FOX_HEAD_DIM
    return pl.pallas_call(
        functools.partial(_fox_attn_kernel, tq=tq),
        grid=(B, FOX_HEADS, S // tq),
        in_specs=[
            pl.BlockSpec((1, tq, Dh), lambda b, h, i: (b, i, h)),
            pl.BlockSpec((1, S, Dh), lambda b, h, i: (b, 0, h)),
            pl.BlockSpec((1, S, Dh), lambda b, h, i: (b, 0, h)),
            pl.BlockSpec((1, 8, S), lambda b, h, i: (b, 0, 0)),
        ],
        out_specs=pl.BlockSpec((1, tq, Dh), lambda b, h, i: (b, i, h)),
        out_shape=jax.ShapeDtypeStruct((B, S, W), BF16),
        compiler_params=_params(("parallel", "parallel", "arbitrary")),
        name="fox_attn",
    )(qn, kn, vb, ct)


SSD_BLOCK = 1792
SSD_SHIFT = 4


def _ssd_kernel(u_ref, cw_ref, cb_ref, dtb_ref, alog_ref, dsk_ref, nw_ref,
                o_ref, halo_ref, st_ref, *, lc):
    @pl.when(pl.program_id(1) == 0)
    def _():
        halo_ref[...] = jnp.zeros_like(halo_ref)
        st_ref[...] = jnp.zeros_like(st_ref)

    P, N = SSD_HEAD_DIM, SSD_STATE
    W = SSD_HEADS * P
    u = pltpu.roll(u_ref[0], SSD_BLOCK - SSD_SHIFT, 1)
    z = u[:, 0:W]
    x = u[:, W:W + 1024]
    dt_raw = u[:, W + 1024:W + 1152]
    xc = _causal_conv(x, halo_ref[...], cw_ref[...], cb_ref[...], 4)
    halo_ref[...] = x[lc - 8:]
    xc = xc * _sigmoid(xc)
    xs = xc[:, :W]
    dt = _softplus(dt_raw + dtb_ref[...])
    a_neg = -jnp.exp(alog_ref[...])
    cum = _dot_exact_lhs(_tril_ones(lc), dt * a_neg)
    cum_t = cum.T
    cum_last = cum[lc - 1:lc]
    r = lax.broadcasted_iota(jnp.int32, (lc, lc), 0)
    c = lax.broadcasted_iota(jnp.int32, (lc, lc), 1)
    tril = c <= r
    rep = SSD_HEADS // SSD_GROUPS
    ys = []
    cb_g = [None] * SSD_GROUPS
    for h in range(SSD_HEADS):
        g = h // rep
        bm = _bf(xc[:, W + g * N:W + (g + 1) * N])
        cm = _bf(xc[:, W + SSD_GROUPS * N + g * N:W + SSD_GROUPS * N + (g + 1) * N])
        if cb_g[g] is None:
            cb_g[g] = _dot_nt(cm, bm)
        cum_col = cum[:, h:h + 1]
        dt_col = dt[:, h:h + 1]
        lmat = jnp.where(tril, jnp.exp(cum_col - cum_t[h:h + 1, :]), 0.0)
        xh = xs[:, h * P:(h + 1) * P]
        xdt = xh * dt_col
        y = _dot(_bf(cb_g[g] * lmat), _bf(xdt))
        st = st_ref[h]
        y = y + _dot(cm, _bf(st)) * jnp.exp(cum_col)
        last = cum_last[:, h:h + 1]
        xdec = xdt * jnp.exp(last - cum_col)
        st_ref[h] = jnp.exp(last) * st + _dot_tn(bm, _bf(xdec))
        ys.append(y + xh * dsk_ref[:, h * P:(h + 1) * P])
    y = jnp.concatenate(ys, axis=1)
    y = y * (z * _sigmoid(z))
    gw = W // SSD_GROUPS
    outs = []
    for g in range(SSD_GROUPS):
        outs.append(_rms(y[:, g * gw:(g + 1) * gw], nw_ref[:, g * gw:(g + 1) * gw]))
    o_ref[0] = _bf(jnp.concatenate(outs, axis=1))


def _ssd(proj3, cw, cb, dtb, alog, dsk, nw, lc):
    B, S, _ = proj3.shape
    W = BRANCH_WIDTH
    row = lambda b, c: (0, 0)
    return pl.pallas_call(
        functools.partial(_ssd_kernel, lc=lc),
        grid=(B, S // lc),
        in_specs=[
            pl.BlockSpec((1, lc, SSD_BLOCK), lambda b, c: (b, c, COL_C // SSD_BLOCK)),
            pl.BlockSpec((4, 1024), row),
            pl.BlockSpec((1, 1024), row),
            pl.BlockSpec((1, 128), row),
            pl.BlockSpec((1, 128), row),
            pl.BlockSpec((1, W), row),
            pl.BlockSpec((1, W), row),
        ],
        out_specs=pl.BlockSpec((1, lc, W), lambda b, c: (b, c, 0)),
        out_shape=jax.ShapeDtypeStruct((B, S, W), BF16),
        scratch_shapes=[pltpu.VMEM((8, 1024), F32),
                        pltpu.VMEM((SSD_HEADS, SSD_STATE, SSD_HEAD_DIM), F32)],
        compiler_params=_params(("parallel", "arbitrary")),
        name="ssd",
    )(proj3, cw, cb, dtb, alog, dsk, nw)


RWKV_BLOCK = 2048
RWKV_SHIFT = 12
RWKV_CHUNK = 64


def _rwkv_kernel(u_ref, mu_ref, w0_ref, w2_ref, a0_ref, a2_ref, g2_ref, kkw_ref, kaw_ref, rk_ref,
                 lnw_ref, lnb_ref, seg_ref, o_ref, prev_ref, st_ref, *, lc):
    @pl.when(pl.program_id(1) == 0)
    def _():
        prev_ref[...] = jnp.zeros_like(prev_ref)
        st_ref[...] = jnp.zeros_like(st_ref)

    L = RWKV_CHUNK
    C = lc // L
    N = RWKV_HEAD
    HEADS = range(RWKV_HEADS)
    W = RWKV_HEADS * N
    row = lax.broadcasted_iota(jnp.int32, (lc, 1), 0)

    x = pltpu.roll(u_ref[0], RWKV_BLOCK - RWKV_SHIFT, 1)[:, :3 * W + 256]
    sh = jnp.where(row == 0, prev_ref[0:1], pltpu.roll(x, 1, 0))
    prev_ref[0:1] = x[lc - 1:lc]
    u = x + (sh - x) * mu_ref[...]
    r, k, v = u[:, 0:W], u[:, W:2 * W], u[:, 2 * W:3 * W]
    wl, al, gl = u[:, 3 * W:3 * W + 64], u[:, 3 * W + 64:3 * W + 128], u[:, 3 * W + 128:3 * W + 256]

    seg = seg_ref[...]

    def seg_sum(x):
        x1 = _bf(x)
        return _dot(x1, seg) + _dot(_bf(x - x1.astype(F32)), seg)

    w = -_softplus(-(w0_ref[...] + _dot(_bf(jnp.tanh(wl)), w2_ref[...]))) - 0.5
    ld = -jnp.exp(w)
    a = _sigmoid(a0_ref[...] + _dot(_bf(al), a2_ref[...]))
    g = _dot(_bf(_sigmoid(gl)), g2_ref[...])
    kk = k * kkw_ref[...]
    kp = k * (1.0 + (a - 1.0) * kaw_ref[...])
    sums = seg_sum(jnp.concatenate([kk * kk, r * kp * rk_ref[...]], axis=0))
    kk = kk / jnp.maximum(jnp.sqrt(sums[:lc]), 1e-12)
    bonus = sums[lc:]
    kka = kk * a

    bi = lax.broadcasted_iota(jnp.int32, (lc, lc), 0)
    bj = lax.broadcasted_iota(jnp.int32, (lc, lc), 1)
    tril_chunks = ((bj <= bi) & (bi // L == bj // L)).astype(BF16)
    cum = _dot_exact_lhs(tril_chunks, ld)
    rs = [slice(j * L, (j + 1) * L) for j in range(C)]
    c_last = [cum[(j + 1) * L - 1:(j + 1) * L] for j in range(C)]
    c_last_rows = jnp.concatenate([jnp.broadcast_to(c_last[j], (L, W)) for j in range(C)], axis=0)
    e_neg = jnp.exp(-cum)
    e_end = jnp.exp(c_last_rows - cum)
    at = _bf(-kk * jnp.exp(cum - ld))
    rt = r * jnp.exp(cum)
    rtb = _bf(rt)
    bt = _bf(kka * e_neg)
    kt = _bf(kp * e_neg)
    bp = _bf(kka * e_end)
    kpe = _bf(kp * e_end)
    w_end = [jnp.exp(c_last[j]) for j in range(C)]
    vb = _bf(v)

    ri = lax.broadcasted_iota(jnp.int32, (L, L), 0)
    ci = lax.broadcasted_iota(jnp.int32, (L, L), 1)
    strict = ci < ri
    incl = ci <= ri
    eye = (ci == ri).astype(F32)
    sl = [slice(h * N, (h + 1) * N) for h in HEADS]
    q = [(rs[j], sl[h]) for j in range(C) for h in HEADS]
    P = range(len(q))

    gm = [_dot_nt(jnp.concatenate([at[q[i]], rtb[q[i]]], axis=0),
                  jnp.concatenate([bt[q[i]], kt[q[i]]], axis=0)) for i in P]
    a_ab = [_bf(jnp.where(strict, gm[i][:L, :L], 0.0)) for i in P]
    a_ak = [_bf(jnp.where(strict, gm[i][:L, L:], 0.0)) for i in P]
    a_rb = [_bf(jnp.where(incl, gm[i][L:, :L], 0.0)) for i in P]
    a_rk = [_bf(jnp.where(incl, gm[i][L:, L:], 0.0)) for i in P]
    pw = [_dot(a_ab[i], a_ab[i]) for i in P]
    tinv = [eye + a_ab[i].astype(F32) for i in P]
    zv = [_dot(a_ak[i], vb[q[i]]) for i in P]
    n_done = 2
    while 2 * n_done < L:
        both = [_dot(_bf(jnp.concatenate([tinv[i], pw[i]], axis=0)), _bf(pw[i])) for i in P]
        tinv = [tinv[i] + both[i][:L] for i in P]
        pw = [both[i][L:] for i in P]
        n_done *= 2
    tinv = [tinv[i] + _dot(_bf(tinv[i]), _bf(pw[i])) for i in P]
    tz = [_dot(_bf(tinv[i]), jnp.concatenate([at[q[i]], _bf(zv[i])], axis=1)) for i in P]
    tzb = [_bf(tz[i]) for i in P]
    mix = [_dot(a_rb[i], tzb[i]) for i in P]
    yv = [mix[i][:, N:] + _dot(a_rk[i], vb[q[i]]) for i in P]
    r2 = [_bf(rt[q[i]] + mix[i][:, :N]) for i in P]
    m_low = [_bf(_dot_tn(tzb[i][:, :N], bp[q[i]])) for i in P]
    c_add = [_dot_tn(jnp.concatenate([tzb[i][:, N:], vb[q[i]]], axis=0),
                     jnp.concatenate([bp[q[i]], kpe[q[i]]], axis=0)) for i in P]
    s = [st_ref[h] for h in HEADS]
    y_rows = []
    for j in range(C):
        sb = [_bf(s[h]) for h in HEADS]
        y_rows.append(jnp.concatenate(
            [_dot_nt(r2[j * RWKV_HEADS + h], sb[h]) + yv[j * RWKV_HEADS + h] for h in HEADS], axis=1))
        s = [s[h] * w_end[j][:, sl[h]] + _dot(sb[h], m_low[j * RWKV_HEADS + h]) + c_add[j * RWKV_HEADS + h]
             for h in HEADS]
    for h in HEADS:
        st_ref[h] = s[h]
    y = jnp.concatenate(y_rows, axis=0)

    inv_n = 1.0 / N
    mean = seg_sum(y) * inv_n
    yc = y - mean
    var = seg_sum(yc * yc) * inv_n
    y = yc * lax.rsqrt(var + RWKV_LN_EPS) * lnw_ref[...] + lnb_ref[...]
    y = y + bonus * v
    o_ref[0] = _bf(y * g)


def _rwkv(proj3, mu, w0, w2, a0, a2, g2, kkw, kaw, rk, lnw, lnb, seg, lc):
    B, S, _ = proj3.shape
    W = BRANCH_WIDTH
    row = lambda b, c: (0, 0)
    vec = pl.BlockSpec((1, W), row)
    return pl.pallas_call(
        functools.partial(_rwkv_kernel, lc=lc),
        grid=(B, S // lc),
        in_specs=[
            pl.BlockSpec((1, lc, RWKV_BLOCK), lambda b, c: (b, c, COL_D // RWKV_BLOCK)),
            pl.BlockSpec((1, 3 * W + 256), row),
            vec, pl.BlockSpec((64, W), row),
            vec, pl.BlockSpec((64, W), row),
            pl.BlockSpec((128, W), row),
            vec, vec, vec, vec, vec,
            pl.BlockSpec((W, W), row),
        ],
        out_specs=pl.BlockSpec((1, lc, W), lambda b, c: (b, c, 0)),
        out_shape=jax.ShapeDtypeStruct((B, S, W), BF16),
        scratch_shapes=[pltpu.VMEM((8, 3 * W + 256), F32),
                        pltpu.VMEM((RWKV_HEADS, RWKV_HEAD, RWKV_HEAD), F32)],
        compiler_params=_params(("parallel", "arbitrary")),
        name="rwkv7",
    )(proj3, mu, w0, w2, a0, a2, g2, kkw, kaw, rk, lnw, lnb, seg)


def _merge_kernel(h_ref, g_ref, ya_ref, yb_ref, yc_ref, yd_ref, wb_ref, wo_ref, o_ref):
    D = D_MODEL
    acc = None
    for n, y_ref in enumerate((ya_ref, yb_ref, yc_ref, yd_ref)):
        t = _sigmoid(g_ref[:, n * D:(n + 1) * D]) * _dot(y_ref[...], wb_ref[n])
        acc = t if acc is None else acc + t
    o_ref[...] = h_ref[...] + _dot(_bf(acc), wo_ref[...])


def _merge(h, proj, ys, wb, wo, tm):
    T, D = h.shape
    W = BRANCH_WIDTH
    yspec = pl.BlockSpec((tm, W), lambda i: (i, 0))
    once = pl.Buffered(1)
    return pl.pallas_call(
        _merge_kernel,
        grid=(T // tm,),
        in_specs=[
            pl.BlockSpec((tm, D), lambda i: (i, 0)),
            pl.BlockSpec((tm, N_BRANCH * D), lambda i: (i, 0)),
            yspec, yspec, yspec, yspec,
            pl.BlockSpec((N_BRANCH, W, D), lambda i: (0, 0, 0), pipeline_mode=once),
            pl.BlockSpec((D, D), lambda i: (0, 0), pipeline_mode=once),
        ],
        out_specs=pl.BlockSpec((tm, D), lambda i: (i, 0)),
        out_shape=jax.ShapeDtypeStruct((T, D), F32),
        compiler_params=_params(("parallel",)),
        name="merge",
    )(h, proj, *ys, wb, wo)


FFN_HALO = 16
FFN_DOWN_TN = 512


def _ffn_up_kernel(hm_ref, hh_ref, nw_ref, wg_ref, wv_ref, cwg_ref, cbg_ref, cwv_ref, cbv_ref,
                   o_ref, hn_ref, *, tm, seq):
    i = pl.program_id(0)

    @pl.when(pl.program_id(1) == 0)
    def _():
        hn_ref[FFN_HALO:, :] = _bf(_rms(hm_ref[...], nw_ref[...]))
        keep = ((i * tm) % seq) != 0
        halo = _rms(hh_ref[...], nw_ref[...])
        hn_ref[0:FFN_HALO, :] = _bf(jnp.where(keep, halo, 0.0))

    hn = hn_ref[...]

    def conv(u, cw, cb):
        return (cb + cw[2:3] * u[FFN_HALO:] + cw[1:2] * u[FFN_HALO - 1:FFN_HALO - 1 + tm]
                + cw[0:1] * u[FFN_HALO - 2:FFN_HALO - 2 + tm])

    ug = conv(_dot(hn, wg_ref[...]), cwg_ref[...], cbg_ref[...])
    uv = conv(_dot(hn, wv_ref[...]), cwv_ref[...], cbv_ref[...])
    o_ref[...] = _bf(_gelu(ug) * uv)


def _ffn_down_kernel(a_ref, w_ref, h_ref, o_ref):
    o_ref[...] = h_ref[...] + _dot(a_ref[...], w_ref[...])


def _ffn(h, nw, wup, cw, cb, wd, seq, tm, tf):
    T, D = h.shape
    nf = D_FF // tf
    hb = tm // FFN_HALO
    act = pl.pallas_call(
        functools.partial(_ffn_up_kernel, tm=tm, seq=seq),
        grid=(T // tm, nf),
        in_specs=[
            pl.BlockSpec((tm, D), lambda i, f: (i, 0)),
            pl.BlockSpec((FFN_HALO, D), lambda i, f: (jnp.maximum(i * hb - 1, 0), 0)),
            pl.BlockSpec((1, D), lambda i, f: (0, 0)),
            pl.BlockSpec((D, tf), lambda i, f: (0, f)),
            pl.BlockSpec((D, tf), lambda i, f: (0, f + nf)),
            pl.BlockSpec((3, tf), lambda i, f: (0, f)),
            pl.BlockSpec((1, tf), lambda i, f: (0, f)),
            pl.BlockSpec((3, tf), lambda i, f: (0, f + nf)),
            pl.BlockSpec((1, tf), lambda i, f: (0, f + nf)),
        ],
        out_specs=pl.BlockSpec((tm, tf), lambda i, f: (i, f)),
        out_shape=jax.ShapeDtypeStruct((T, D_FF), BF16),
        scratch_shapes=[pltpu.VMEM((tm + FFN_HALO, D), BF16)],
        compiler_params=_params(("parallel", "arbitrary")),
        name="ffn_up",
    )(h, h, nw, wup, wup, cw, cb, cw, cb)
    tn = FFN_DOWN_TN
    return pl.pallas_call(
        _ffn_down_kernel,
        grid=(T // tm, D // tn),
        in_specs=[
            pl.BlockSpec((tm, D_FF), lambda i, j: (i, 0)),
            pl.BlockSpec((D_FF, tn), lambda i, j: (0, j)),
            pl.BlockSpec((tm, tn), lambda i, j: (i, j)),
        ],
        out_specs=pl.BlockSpec((tm, tn), lambda i, j: (i, j)),
        out_shape=jax.ShapeDtypeStruct((T, D), F32),
        compiler_params=_params(("parallel", "arbitrary")),
        name="ffn_down",
    )(act, wd, h)


def _ple_kernel(h_ref, p_ref, nw_ref, wg_ref, wp_ref, o_ref):
    h = h_ref[...]
    gate = _sigmoid(_dot(_bf(_rms(h, nw_ref[...])), wg_ref[...]))
    o_ref[...] = h + gate * _dot(_bf(p_ref[...]), wp_ref[...])


def _ple(h, p, nw, wg, wp, tm):
    T, D = h.shape
    Q = p.shape[1]
    return pl.pallas_call(
        _ple_kernel,
        grid=(T // tm,),
        in_specs=[
            pl.BlockSpec((tm, D), lambda i: (i, 0)),
            pl.BlockSpec((tm, Q), lambda i: (i, 0)),
            pl.BlockSpec((1, D), lambda i: (0, 0)),
            pl.BlockSpec((D, D), lambda i: (0, 0)),
            pl.BlockSpec((Q, D), lambda i: (0, 0)),
        ],
        out_specs=pl.BlockSpec((tm, D), lambda i: (i, 0)),
        out_shape=jax.ShapeDtypeStruct((T, D), F32),
        compiler_params=_params(("parallel",)),
        name="ple",
    )(h, p, nw, wg, wp)


def _pack_w_in(w):
    return _bf(jnp.pad(w, ((0, 0), (0, PROJ_COLS - IN_COLS))))


def _block_diag(w):
    nb, n, _ = w.shape
    eye = jnp.eye(nb, dtype=w.dtype)
    return (eye[:, None, :, None] * w[:, :, None, :]).reshape(nb * n, nb * n)


def _pad_row(v, n):
    return jnp.zeros((1, n), F32).at[0, :v.shape[0]].set(v)


def _pick(n, pref):
    return pref if n % pref == 0 else n


def kernel(x, p, mix_norm, w_in, lru_conv_w, lru_conv_b, lru_wr, lru_br, lru_wi, lru_bi, lru_lambda, fox_fbias, fox_q_gain, fox_k_gain, ssd_conv_w, ssd_conv_b, ssd_dt_bias, ssd_a_log, ssd_d, ssd_norm, rwkv_mu, rwkv_w0, rwkv_w2, rwkv_a0, rwkv_a2, rwkv_g2, rwkv_kk, rwkv_ka, rwkv_rk, rwkv_ln_w, rwkv_ln_b, w_branch, w_out, ffn_norm, w_up, ffn_conv_w, ffn_conv_b, w_down, ple_norm, w_ple_gate, w_ple):
    B, S, D = x.shape
    T = B * S
    h = x.reshape(T, D)
    seg = _bf(_block_diag(jnp.ones((RWKV_HEADS, RWKV_HEAD, RWKV_HEAD), F32)))
    r1 = lambda v: v.reshape(1, -1)
    for i in range(DEPTH):
        proj = _norm_matmul(h, r1(mix_norm[i]), _pack_w_in(w_in[i]), _pick(T, 1024), 1024)
        proj3 = proj.reshape(B, S, PROJ_COLS)
        y_a = _lru(proj3, lru_conv_w[i], r1(lru_conv_b[i]), _bf(_block_diag(lru_wr[i])), r1(lru_br[i]),
                   _bf(_block_diag(lru_wi[i])), r1(lru_bi[i]), r1(lru_lambda[i]), _pick(S, 256))
        qn, kn, vb, ct = _fox_prep(proj3, r1(fox_q_gain[i]), r1(fox_k_gain[i]),
                                   _pad_row(fox_fbias[i], 128), _pick(S, 512))
        y_b = _fox_attn(qn, kn, vb, ct, _pick(S, 512))
        y_c = _ssd(proj3, ssd_conv_w[i], r1(ssd_conv_b[i]), _pad_row(ssd_dt_bias[i], 128),
                   _pad_row(ssd_a_log[i], 128), r1(jnp.repeat(ssd_d[i], SSD_HEAD_DIM)),
                   r1(ssd_norm[i]), _pick(S, 256))
        y_d = _rwkv(proj3, r1(rwkv_mu[i]), r1(rwkv_w0[i]), _bf(rwkv_w2[i]), r1(rwkv_a0[i]),
                    _bf(rwkv_a2[i]), _bf(rwkv_g2[i]), r1(rwkv_kk[i]), r1(rwkv_ka[i]),
                    r1(rwkv_rk[i]), r1(rwkv_ln_w[i]), r1(rwkv_ln_b[i]), seg, _pick(S, 256))
        ys = [y.reshape(T, BRANCH_WIDTH) for y in (y_a, y_b, y_c, y_d)]
        h = _merge(h, proj, ys, _bf(w_branch[i]), _bf(w_out[i]), _pick(T, 256))
        h = _ffn(h, r1(ffn_norm[i]), _bf(w_up[i]), ffn_conv_w[i], r1(ffn_conv_b[i]), _bf(w_down[i]),
                 S, _pick(S, 1024), 512)
        h = _ple(h, p[i].reshape(T, PLE_DIM), r1(ple_norm[i]), _bf(w_ple_gate[i]), _bf(w_ple[i]),
                 _pick(T, 512))
    return h.reshape(B, S, D)
```
